```python
import jax
import jax.numpy as jnp
from jax import lax
import numpy as np

D_MODEL = 4096
BATCH = 4
SEQ = 4096
DEPTH = 4
DEC_BATCH = 32
DEC_SEQ = 32
PAST_LEN = 1024

CHUNK = 64
EPS = 1e-6
A_DK = 128
A_DV = 256
A_HEADS = D_MODEL // 512
A_QK = A_HEADS * A_DK
A_V = A_HEADS * A_DV
ROPE_BASE = 10000.0
B_DK = 128
B_DV = 128
B_HEADS = D_MODEL // 256
B_QK = B_HEADS * B_DK
B_V = B_HEADS * B_DV
CONV_W = 4
CONV_CH = 2 * B_QK + B_V
EVEN_IN = 2 * A_QK + 2 * A_V + CONV_CH + B_V + 2 * B_HEADS
EVEN_MIX = A_V + B_V
C_HD = 128
C_HEADS = D_MODEL // C_HD
C_WIDTH = C_HEADS * C_HD
BAND_PREV = 8
REL_CLIP = 128
D_FF = 4 * D_MODEL
N_EVEN = (DEPTH + 1) // 2
N_ODD = DEPTH // 2

kernel_name = 'hybrid_streaming_encoder_step'


def rms_norm(x, g):
    xf = x.astype(jnp.float32)
    y = xf * lax.rsqrt(jnp.mean(xf * xf, axis=-1, keepdims=True) + EPS)
    return (y * g.astype(jnp.float32)).astype(x.dtype)


def modulate(h, shift, scale):
    return h * (1 + scale[:, None, :]) + shift[:, None, :]


def l2_normalize(x):
    return x * lax.rsqrt(jnp.sum(x * x, axis=-1, keepdims=True) + EPS)


def rope(x, pos):
    half = x.shape[-1] // 2
    inv_freq = ROPE_BASE ** (-jnp.arange(half, dtype=jnp.float32) / half)
    ang = pos.astype(jnp.float32)[:, None] * inv_freq[None, :]
    cos = jnp.cos(ang)[None, :, None, :]
    sin = jnp.sin(ang)[None, :, None, :]
    x1, x2 = x[..., :half], x[..., half:]
    return jnp.concatenate([x1 * cos - x2 * sin, x2 * cos + x1 * sin], axis=-1)


def causal_conv(u, buf, w):
    L = u.shape[1]
    full = jnp.concatenate([buf.astype(u.dtype), u], axis=1)
    out = full[:, 0:L] * w[0]
    for i in range(1, CONV_W):
        out = out + full[:, i:i + L] * w[i]
    return out, full[:, L:]


def retention(q, k, v, state, log_gamma):
    B, L, H, dk = q.shape
    dv = v.shape[-1]
    C = min(CHUNK, L)
    N = L // C
    qc = q.reshape(B, N, C, H, dk)
    kc = k.reshape(B, N, C, H, dk)
    vc = v.reshape(B, N, C, H, dv)
    idx = jnp.arange(C, dtype=jnp.float32)
    rel = idx[:, None] - idx[None, :]
    decay_mask = jnp.where((rel >= 0)[None],
                           jnp.exp(jnp.maximum(rel, 0.0)[None] * log_gamma[:, None, None]), 0.0)
    scores = jnp.einsum('bnqhd,bnkhd->bnhqk', qc, kc) * decay_mask
    o_intra = jnp.einsum('bnhqk,bnkhe->bnqhe', scores, vc)
    q_decay = jnp.exp((idx[:, None] + 1.0) * log_gamma[None, :])
    k_decay = jnp.exp((C - 1.0 - idx)[:, None] * log_gamma[None, :])
    chunk_decay = jnp.exp(C * log_gamma)

    def step(S, xs):
        q_i, k_i, v_i = xs
        o = jnp.einsum('bqhd,bhde->bqhe', q_i * q_decay[None, :, :, None], S)
        S = S * chunk_decay[None, :, None, None] + jnp.einsum(
            'bkhd,bkhe->bhde', k_i * k_decay[None, :, :, None], v_i)
        return S, o

    S, o_inter = lax.scan(step, state, (jnp.moveaxis(qc, 1, 0), jnp.moveaxis(kc, 1, 0), jnp.moveaxis(vc, 1, 0)))
    o = o_intra + jnp.moveaxis(o_inter, 0, 1)
    return o.reshape(B, L, H, dv), S


def gated_delta(q, k, v, beta, g, state):
    B, L, H, dk = q.shape
    dv = v.shape[-1]
    C = min(CHUNK, L)
    N = L // C

    def chunks(x):
        return jnp.moveaxis(x.reshape((B, N, C, H) + x.shape[3:]), 3, 2)

    qc, kc, vc, bc = chunks(q), chunks(k), chunks(v), chunks(beta)
    gc = jnp.cumsum(chunks(g), axis=-1)
    idx = jnp.arange(C)
    lower = idx[:, None] >= idx[None, :]
    strict = idx[:, None] > idx[None, :]
    diff = gc[..., :, None] - gc[..., None, :]
    decay_mask = jnp.where(lower, jnp.exp(jnp.where(lower, diff, 0.0)), 0.0)
    kb = kc * bc[..., None]
    a = jnp.where(strict, jnp.einsum('bnhid,bnhjd->bnhij', kb, kc) * decay_mask, 0.0)
    tri = a + jnp.eye(C, dtype=a.dtype)
    u = lax.linalg.triangular_solve(tri, vc * bc[..., None], left_side=True, lower=True, unit_diagonal=True)
    w = lax.linalg.triangular_solve(tri, kb * jnp.exp(gc)[..., None], left_side=True, lower=True,
                                    unit_diagonal=True)
    qk = jnp.einsum('bnhid,bnhjd->bnhij', qc, kc) * decay_mask
    q_dec = qc * jnp.exp(gc)[..., None]
    last = gc[..., -1:]
    k_dec = kc * jnp.exp(last - gc)[..., None]
    chunk_decay = jnp.exp(last)[..., None]

    def step(S, xs):
        u_i, w_i, qk_i, qd_i, kd_i, cd_i = xs
        v_new = u_i - jnp.einsum('bhcd,bhde->bhce', w_i, S)
        o = jnp.einsum('bhcd,bhde->bhce', qd_i, S) + jnp.einsum('bhij,bhje->bhie', qk_i, v_new)
        S = S * cd_i + jnp.einsum('bhcd,bhce->bhde', kd_i, v_new)
        return S, o

    xs = (jnp.moveaxis(u, 1, 0), jnp.moveaxis(w, 1, 0), jnp.moveaxis(qk, 1, 0),
          jnp.moveaxis(q_dec, 1, 0), jnp.moveaxis(k_dec, 1, 0), jnp.moveaxis(chunk_decay, 1, 0))
    S, o = lax.scan(step, state, xs)
    o = jnp.transpose(o, (1, 0, 3, 2, 4)).reshape(B, L, H, dv)
    return o, S


def rel_bias_lookup(rel_bias, rel):
    return jnp.transpose(rel_bias[jnp.clip(rel, -REL_CLIP, REL_CLIP) + REL_CLIP], (2, 0, 1))


def attend(q, k, v, bias, valid):
    s = jnp.einsum('bqhd,bkhd->bhqk', q, k).astype(jnp.float32) * (q.shape[-1] ** -0.5)
    s = s + bias.astype(jnp.float32)[None]
    if valid is not None:
        s = jnp.where(valid, s, -jnp.inf)
    p = jax.nn.softmax(s, axis=-1).astype(v.dtype)
    return jnp.einsum('bhqk,bkhd->bqhd', p, v)


def band_attention_prompt(q, k, v, rel_bias):
    B, L, H, d = q.shape
    N = L // CHUNK
    P = BAND_PREV * CHUNK
    W = P + CHUNK
    pad = ((0, 0), (P, 0), (0, 0), (0, 0))
    kp = jnp.pad(k, pad)
    vp = jnp.pad(v, pad)
    t = jnp.arange(CHUNK)
    j = jnp.arange(W)
    bias = rel_bias_lookup(rel_bias, t[:, None] + P - j[None, :])
    qc = jnp.moveaxis(q.reshape(B, N, CHUNK, H, d), 1, 0)

    def one_chunk(xs):
        i, q_i = xs
        start = i * CHUNK
        k_i = lax.dynamic_slice_in_dim(kp, start, W, axis=1)
        v_i = lax.dynamic_slice_in_dim(vp, start, W, axis=1)
        valid = (start - P + j) >= 0
        return attend(q_i, k_i, v_i, bias, valid)

    o = lax.map(one_chunk, (jnp.arange(N), qc))
    return jnp.moveaxis(o, 0, 1).reshape(B, L, H * d)


def band_attention_sample(q, k, v, cache_k, cache_v, rel_bias):
    B, L, H, d = q.shape
    Lc = cache_k.shape[1]
    kk = jnp.concatenate([cache_k.astype(k.dtype), k], axis=1)
    vv = jnp.concatenate([cache_v.astype(v.dtype), v], axis=1)
    t = jnp.arange(L)
    j = jnp.arange(Lc + L)
    bias = rel_bias_lookup(rel_bias, t[:, None] + Lc - j[None, :])
    return attend(q, kk, vv, bias, None).reshape(B, L, H * d)


def mixer_even(h, pos0, state, w_in, w_out, gn_g, conv_w, a_log, dt_bias, dn_g):
    B, L, _ = h.shape
    f32 = jnp.float32
    if state is None:
        ret_s = jnp.zeros((B, A_HEADS, A_DK, A_DV), f32)
        delta_s = jnp.zeros((B, B_HEADS, B_DK, B_DV), f32)
        conv_buf = jnp.zeros((B, CONV_W - 1, CONV_CH), h.dtype)
    else:
        ret_s, delta_s, conv_buf = state
    sizes = [A_QK, A_QK, A_V, A_V, CONV_CH, B_V, B_HEADS, B_HEADS]
    cuts = np.cumsum(sizes)[:-1].tolist()
    a_q, a_k, a_v, a_g, b_qkv, b_z, b_b, b_a = jnp.split(h @ w_in, cuts, axis=-1)
    pos = pos0 + jnp.arange(L)
    q = rope(a_q.reshape(B, L, A_HEADS, A_DK).astype(f32), pos)
    k = rope(a_k.reshape(B, L, A_HEADS, A_DK).astype(f32), pos) * (A_DK ** -0.5)
    v = a_v.reshape(B, L, A_HEADS, A_DV).astype(f32)
    log_gamma = jnp.log1p(-jnp.exp2(-5.0 - jnp.arange(A_HEADS, dtype=f32)))
    o_a, ret_new = retention(q, k, v, ret_s.astype(f32), log_gamma)
    mu = jnp.mean(o_a, axis=-1, keepdims=True)
    var = jnp.mean(jnp.square(o_a - mu), axis=-1, keepdims=True)
    o_a = ((o_a - mu) * lax.rsqrt(var + EPS)).reshape(B, L, A_V) * gn_g.astype(f32)
    y_a = jax.nn.silu(a_g.astype(f32)) * o_a
    conv_out, conv_new = causal_conv(b_qkv, conv_buf, conv_w)
    conv_out = jax.nn.silu(conv_out.astype(f32))
    cq, ck, cv = jnp.split(conv_out, [B_QK, 2 * B_QK], axis=-1)
    bq = l2_normalize(cq.reshape(B, L, B_HEADS, B_DK)) * (B_DK ** -0.5)
    bk = l2_normalize(ck.reshape(B, L, B_HEADS, B_DK))
    bv = cv.reshape(B, L, B_HEADS, B_DV)
    beta = jax.nn.sigmoid(b_b.astype(f32))
    g = -jnp.exp(a_log.astype(f32)) * jax.nn.softplus(b_a.astype(f32) + dt_bias.astype(f32))
    o_b, delta_new = gated_delta(bq, bk, bv, beta, g, delta_s.astype(f32))
    o_b = o_b * lax.rsqrt(jnp.mean(o_b * o_b, axis=-1, keepdims=True) + EPS) * dn_g.astype(f32)
    y_b = o_b.reshape(B, L, B_V) * jax.nn.silu(b_z.astype(f32))
    y = jnp.concatenate([y_a, y_b], axis=-1).astype(h.dtype) @ w_out
    return y, (ret_new.astype(h.dtype), delta_new.astype(h.dtype), conv_new)


def mixer_odd(h, state, w_in, w_out, rel_bias):
    B, L, _ = h.shape
    q, k, v = [t.reshape(B, L, C_HEADS, C_HD) for t in jnp.split(h @ w_in, 3, axis=-1)]
    if state is None:
        o = band_attention_prompt(q, k, v, rel_bias)
        keep = min(BAND_PREV * CHUNK, L)
        new = (k[:, L - keep:], v[:, L - keep:])
    else:
        cache_k, cache_v = state
        o = band_attention_sample(q, k, v, cache_k, cache_v, rel_bias)
        new = (k, v)
    return o @ w_out, new


def sq_relu_mlp(h, w1, w2):
    return jnp.square(jax.nn.relu(h @ w1)) @ w2


def trunk(x, c, pos0, states, norm_mix_g, norm_ffn_g, w_mod, b_mod, w_in_even, w_out_even, a_norm_g,
          b_conv_w, b_a_log, b_dt_bias, b_norm_g, w_in_odd, w_out_odd, c_rel_bias, w_ff1, w_ff2,
          final_norm_g):
    new_states = []
    c_act = jax.nn.silu(c)
    for l in range(DEPTH):
        j = l // 2
        mod = c_act @ w_mod[l] + b_mod[l]
        sh1, sc1, g1, sh2, sc2, g2 = jnp.split(mod, 6, axis=-1)
        h = modulate(rms_norm(x, norm_mix_g[l]), sh1, sc1)
        if l % 2 == 0:
            y, st = mixer_even(h, pos0, states[l], w_in_even[j], w_out_even[j], a_norm_g[j], b_conv_w[j],
                               b_a_log[j], b_dt_bias[j], b_norm_g[j])
        else:
            y, st = mixer_odd(h, states[l], w_in_odd[j], w_out_odd[j], c_rel_bias[j])
        x = x + g1[:, None, :] * y
        h = modulate(rms_norm(x, norm_ffn_g[l]), sh2, sc2)
        x = x + g2[:, None, :] * sq_relu_mlp(h, w_ff1[l], w_ff2[l])
        new_states.append(st)
    return rms_norm(x, final_norm_g), new_states


def setup_inputs(seed: int = 0) -> dict:
    key = jax.random.key(seed)
    keys = jax.random.split(key, 40)
    counter = [0]

    def next_key():
        k = keys[counter[0]]
        counter[0] += 1
        return k

    def nrm(shape, scale):
        return jax.random.normal(next_key(), shape, jnp.float32) * scale

    def gain(shape):
        return 1.0 + nrm(shape, 0.05)

    lc = min(BAND_PREV * CHUNK, PAST_LEN)
    inp = {}
    inp['x_prompt'] = nrm((BATCH, SEQ, D_MODEL), 1.0)
    inp['x_sample'] = nrm((DEC_BATCH, DEC_SEQ, D_MODEL), 1.0)
    inp['c_prompt'] = nrm((BATCH, D_MODEL), 1.0)
    inp['c_sample'] = nrm((DEC_BATCH, D_MODEL), 1.0)
    for l in range(DEPTH):
        if l % 2 == 0:
            inp['state_ret_%d' % l] = nrm((DEC_BATCH, A_HEADS, A_DK, A_DV), 1.0)
            inp['state_delta_%d' % l] = nrm((DEC_BATCH, B_HEADS, B_DK, B_DV), 0.1)
            inp['state_conv_%d' % l] = nrm((DEC_BATCH, CONV_W - 1, CONV_CH), 1.0)
        else:
            inp['cache_k_%d' % l] = nrm((DEC_BATCH, lc, C_HEADS, C_HD), 1.0)
            inp['cache_v_%d' % l] = nrm((DEC_BATCH, lc, C_HEADS, C_HD), 1.0)
    inp['norm_mix_g'] = gain((DEPTH, D_MODEL))
    inp['norm_ffn_g'] = gain((DEPTH, D_MODEL))
    inp['w_mod'] = nrm((DEPTH, D_MODEL, 6 * D_MODEL), 0.5 * D_MODEL ** -0.5)
    inp['b_mod'] = nrm((DEPTH, 6 * D_MODEL), 0.02)
    inp['w_in_even'] = nrm((N_EVEN, D_MODEL, EVEN_IN), D_MODEL ** -0.5)
    inp['w_out_even'] = nrm((N_EVEN, EVEN_MIX, D_MODEL), EVEN_MIX ** -0.5)
    inp['a_norm_g'] = gain((N_EVEN, A_V))
    inp['b_conv_w'] = nrm((N_EVEN, CONV_W, CONV_CH), CONV_W ** -0.5)
    inp['b_a_log'] = jnp.log(jax.random.uniform(next_key(), (N_EVEN, B_HEADS), jnp.float32, 1.0, 16.0))
    inp['b_dt_bias'] = nrm((N_EVEN, B_HEADS), 0.1)
    inp['b_norm_g'] = gain((N_EVEN, B_DV))
    inp['w_in_odd'] = nrm((N_ODD, D_MODEL, 3 * C_WIDTH), D_MODEL ** -0.5)
    inp['w_out_odd'] = nrm((N_ODD, C_WIDTH, D_MODEL), C_WIDTH ** -0.5)
    inp['c_rel_bias'] = nrm((N_ODD, 2 * REL_CLIP + 1, C_HEADS), 0.5)
    inp['w_ff1'] = nrm((DEPTH, D_MODEL, D_FF), D_MODEL ** -0.5)
    inp['w_ff2'] = nrm((DEPTH, D_FF, D_MODEL), D_FF ** -0.5)
    inp['final_norm_g'] = gain((D_MODEL,))
    return inp


def reference(x_prompt, x_sample, c_prompt, c_sample, state_ret_0, state_delta_0, state_conv_0, cache_k_1,
              cache_v_1, state_ret_2, state_delta_2, state_conv_2, cache_k_3, cache_v_3, norm_mix_g, norm_ffn_g,
              w_mod, b_mod, w_in_even, w_out_even, a_norm_g, b_conv_w, b_a_log, b_dt_bias, b_norm_g, w_in_odd,
              w_out_odd, c_rel_bias, w_ff1, w_ff2, final_norm_g):
    weights = (norm_mix_g, norm_ffn_g, w_mod, b_mod, w_in_even, w_out_even, a_norm_g, b_conv_w, b_a_log,
               b_dt_bias, b_norm_g, w_in_odd, w_out_odd, c_rel_bias, w_ff1, w_ff2, final_norm_g)
    y_prompt, p_states = trunk(x_prompt, c_prompt, 0, [None] * DEPTH, *weights)
    sample_states = [(state_ret_0, state_delta_0, state_conv_0), (cache_k_1, cache_v_1),
                     (state_ret_2, state_delta_2, state_conv_2), (cache_k_3, cache_v_3)]
    y_sample, s_states = trunk(x_sample, c_sample, PAST_LEN, sample_states, *weights)
    (p_ret_0, p_delta_0, p_conv_0), (p_k_1, p_v_1), (p_ret_2, p_delta_2, p_conv_2), (p_k_3, p_v_3) = p_states
    (s_ret_0, s_delta_0, s_conv_0), (s_k_1, s_v_1), (s_ret_2, s_delta_2, s_conv_2), (s_k_3, s_v_3) = s_states
    return (y_prompt, y_sample,
            p_ret_0, s_ret_0, p_delta_0, s_delta_0, p_conv_0, s_conv_0,
            p_k_1, s_k_1, p_v_1, s_v_1,
            p_ret_2, s_ret_2, p_delta_2, s_delta_2, p_conv_2, s_conv_2,
            p_k_3, s_k_3, p_v_3, s_v_3)
```

```python
import functools

import jax
import jax.numpy as jnp
from jax import lax
from jax.experimental import pallas as pl
from jax.experimental.pallas import tpu as pltpu

F32 = jnp.float32
BF16 = jnp.bfloat16

D_MODEL = 4096
BATCH = 4
SEQ = 4096
DEPTH = 4
DEC_BATCH = 32
DEC_SEQ = 32
PAST_LEN = 1024
CHUNK = 64
EPS = 1e-6
A_DK = 128
A_DV = 256
A_HEADS = 8
A_QK = A_HEADS * A_DK
A_V = A_HEADS * A_DV
ROPE_BASE = 10000.0
B_DK = 128
B_DV = 128
B_HEADS = 16
B_QK = B_HEADS * B_DK
B_V = B_HEADS * B_DV
CONV_W = 4
CONV_CH = 2 * B_QK + B_V
EVEN_MAIN = 2 * A_QK + 2 * A_V + CONV_CH + B_V
EVEN_IN = EVEN_MAIN + 2 * B_HEADS
C_HD = 128
C_HEADS = 32
C_WIDTH = C_HEADS * C_HD
BAND_PREV = 8
REL_CLIP = 128
D_FF = 4 * D_MODEL

GROUP = 32
M_PROMPT = BATCH * SEQ
M_SAMPLE = DEC_BATCH * DEC_SEQ
M_ALL = M_PROMPT + M_SAMPLE
N_GROUPS = M_ALL // GROUP
N_SEQ = BATCH + DEC_BATCH
N_SEQ_PAD = 40

VMEM_LIMIT = 56 * 1024 * 1024
TM = 1024
TN = 512
LANES = 128
BD = 16


def _cparams(n_axes):
    return pltpu.CompilerParams(dimension_semantics=("arbitrary",) * n_axes, vmem_limit_bytes=VMEM_LIMIT)


def _sigmoid(x):
    return 1.0 / (1.0 + jnp.exp(-x))


def _silu(x):
    return x * _sigmoid(x)


def _dot(a, b):
    return jnp.dot(a.astype(BF16), b.astype(BF16), preferred_element_type=F32)


def _dot_nt(a, b):
    return lax.dot_general(a.astype(BF16), b.astype(BF16), (((1,), (1,)), ((), ())), preferred_element_type=F32)


def _dot_tn(a, b):
    return lax.dot_general(a.astype(BF16), b.astype(BF16), (((0,), (0,)), ((), ())), preferred_element_type=F32)


def _split2(x):
    hi = x.astype(BF16)
    lo = (x - hi.astype(F32)).astype(BF16)
    return hi, lo


def _dot3(a, b):
    ah, al = _split2(a)
    bh, bl = _split2(b)
    d = functools.partial(jnp.dot, preferred_element_type=F32)
    return d(ah, bh) + (d(ah, bl) + d(al, bh))


def _dot_exact_lhs(l_bf16, x):
    d = functools.partial(jnp.dot, preferred_element_type=F32)
    x0 = x.astype(BF16)
    r1 = x - x0.astype(F32)
    x1 = r1.astype(BF16)
    x2 = (r1 - x1.astype(F32)).astype(BF16)
    return d(l_bf16, x0) + (d(l_bf16, x1) + d(l_bf16, x2))


def _mod_kernel(c_ref, w_ref, b_ref, o_ref):
    c = c_ref[...]
    o_ref[...] = _dot(_silu(c), w_ref[...]) + b_ref[...]


def _modulation(c_all, w_mod, b_mod):
    n_out = 6 * D_MODEL
    return pl.pallas_call(
        _mod_kernel,
        out_shape=jax.ShapeDtypeStruct((DEPTH, N_SEQ_PAD, n_out), F32),
        grid=(DEPTH, n_out // TN),
        in_specs=[
            pl.BlockSpec((N_SEQ_PAD, D_MODEL), lambda l, j: (0, 0)),
            pl.BlockSpec((None, D_MODEL, TN), lambda l, j: (l, 0, j)),
            pl.BlockSpec((None, 1, TN), lambda l, j: (l, 0, j)),
        ],
        out_specs=pl.BlockSpec((None, N_SEQ_PAD, TN), lambda l, j: (l, 0, j)),
        compiler_params=_cparams(2),
        name="modulation",
    )(c_all, w_mod, b_mod.reshape(DEPTH, 1, n_out))


NM_GROUPS = 8


def _normmod_kernel(x_ref, g_ref, sh_ref, sc_ref, o_ref):
    x = x_ref[...]
    y = x * lax.rsqrt(jnp.mean(x * x, axis=-1, keepdims=True) + EPS) * g_ref[...]
    h = y * (1.0 + sc_ref[...]) + sh_ref[...]
    o_ref[...] = h.reshape(o_ref.shape).astype(o_ref.dtype)


def _norm_modulate(x3, g_row, gm, shift_col, scale_col):
    tg = NM_GROUPS
    return pl.pallas_call(
        _normmod_kernel,
        out_shape=jax.ShapeDtypeStruct((M_ALL, D_MODEL), BF16),
        grid=(N_GROUPS // tg,),
        in_specs=[
            pl.BlockSpec((tg, GROUP, D_MODEL), lambda i: (i, 0, 0)),
            pl.BlockSpec((1, D_MODEL), lambda i: (0, 0)),
            pl.BlockSpec((tg, 1, D_MODEL), lambda i: (i, 0, shift_col)),
            pl.BlockSpec((tg, 1, D_MODEL), lambda i: (i, 0, scale_col)),
        ],
        out_specs=pl.BlockSpec((tg * GROUP, D_MODEL), lambda i: (i, 0)),
        compiler_params=_cparams(1),
        name="norm_modulate",
    )(x3, g_row, gm, gm)


def _final_norm_kernel(x_ref, g_ref, o_ref):
    x = x_ref[...]
    y = x * lax.rsqrt(jnp.mean(x * x, axis=-1, keepdims=True) + EPS) * g_ref[...]
    o_ref[...] = y.reshape(o_ref.shape)


def _final_norm(x3, g_row, group0, n_groups):
    tg = NM_GROUPS
    return pl.pallas_call(
        _final_norm_kernel,
        out_shape=jax.ShapeDtypeStruct((n_groups * GROUP, D_MODEL), F32),
        grid=(n_groups // tg,),
        in_specs=[
            pl.BlockSpec((tg, GROUP, D_MODEL), lambda i: (group0 // tg + i, 0, 0)),
            pl.BlockSpec((1, D_MODEL), lambda i: (0, 0)),
        ],
        out_specs=pl.BlockSpec((tg * GROUP, D_MODEL), lambda i: (i, 0)),
        compiler_params=_cparams(1),
        name="final_norm",
    )(x3, g_row)


def _cast_weight(w_ref, wb_ref):
    @pl.when(pl.program_id(1) == 0)
    def _():
        wb_ref[...] = w_ref[...].astype(BF16)


def _mm_kernel(a_ref, w_ref, o_ref, wb_ref, *, relu2):
    _cast_weight(w_ref, wb_ref)
    acc = jnp.dot(a_ref[...], wb_ref[...], preferred_element_type=F32)
    if relu2:
        acc = jnp.square(jnp.maximum(acc, 0.0))
    o_ref[...] = acc.astype(o_ref.dtype)


def _matmul(a, w, layer, n_cols, *, relu2, out_dtype):
    k = a.shape[1]
    return pl.pallas_call(
        functools.partial(_mm_kernel, relu2=relu2),
        out_shape=jax.ShapeDtypeStruct((M_ALL, n_cols), out_dtype),
        grid=(n_cols // TN, M_ALL // TM),
        in_specs=[
            pl.BlockSpec((TM, k), lambda j, i: (i, 0)),
            pl.BlockSpec((None, k, TN), lambda j, i: (layer, 0, j)),
        ],
        out_specs=pl.BlockSpec((TM, TN), lambda j, i: (i, j)),
        scratch_shapes=[pltpu.VMEM((k, TN), BF16)],
        compiler_params=_cparams(2),
        name="proj_relu2" if relu2 else "proj",
    )(a, w)


def _mm_res_kernel(a_ref, w_ref, x_ref, g_ref, o_ref, wb_ref):
    _cast_weight(w_ref, wb_ref)
    acc = jnp.dot(a_ref[...], wb_ref[...], preferred_element_type=F32)
    o_ref[...] = x_ref[...] + g_ref[...] * acc.reshape(x_ref.shape)


def _matmul_residual(a, w, layer, x3, gm, gate_col, *, kblk=0, kc=D_MODEL):
    tg = TM // GROUP
    nj = D_MODEL // TN
    return pl.pallas_call(
        _mm_res_kernel,
        out_shape=jax.ShapeDtypeStruct(x3.shape, F32),
        grid=(nj, M_ALL // TM),
        in_specs=[
            pl.BlockSpec((TM, kc), lambda j, i: (i, kblk)),
            pl.BlockSpec((None, kc, TN), lambda j, i: (layer, kblk, j)),
            pl.BlockSpec((tg, GROUP, TN), lambda j, i: (i, 0, j)),
            pl.BlockSpec((tg, 1, TN), lambda j, i: (i, 0, gate_col * nj + j)),
        ],
        out_specs=pl.BlockSpec((tg, GROUP, TN), lambda j, i: (i, 0, j)),
        scratch_shapes=[pltpu.VMEM((kc, TN), BF16)],
        input_output_aliases={2: 0},
        compiler_params=_cparams(2),
        name="proj_residual",
    )(a, w, x3, gm)


def _bg_kernel(a_ref, w_ref, alog_ref, dtb_ref, o_ref):
    acc = jnp.dot(a_ref[...], w_ref[...], preferred_element_type=F32)
    lane = lax.broadcasted_iota(jnp.int32, acc.shape, 1)
    z = acc + dtb_ref[...]
    softplus = jnp.maximum(z, 0.0) + jnp.log1p(jnp.exp(-jnp.abs(z)))
    g = -jnp.exp(alog_ref[...]) * softplus
    o_ref[...] = jnp.where(lane < B_HEADS, _sigmoid(acc), g)


def _beta_gate(a, w_small, alog_row, dtb_row):
    return pl.pallas_call(
        _bg_kernel,
        out_shape=jax.ShapeDtypeStruct((M_ALL, LANES), F32),
        grid=(M_ALL // TM,),
        in_specs=[
            pl.BlockSpec((TM, D_MODEL), lambda i: (i, 0)),
            pl.BlockSpec((D_MODEL, LANES), lambda i: (0, 0)),
            pl.BlockSpec((1, LANES), lambda i: (0, 0)),
            pl.BlockSpec((1, LANES), lambda i: (0, 0)),
        ],
        out_specs=pl.BlockSpec((TM, LANES), lambda i: (i, 0)),
        compiler_params=_cparams(1),
        name="beta_gate",
    )(a, w_small, alog_row, dtb_row)


class _Seg:
    def __init__(self, batch, length, row0, step, chunk):
        self.batch, self.length, self.row0, self.step, self.chunk = batch, length, row0, step, chunk
        self.nblk = length // step
        self.nch = step // chunk

    def rb(self, b, n):
        return (self.row0 + b * self.length) // self.step + n


SEG_PROMPT = _Seg(BATCH, SEQ, 0, 256, CHUNK)
SEG_SAMPLE = _Seg(DEC_BATCH, DEC_SEQ, M_PROMPT, DEC_SEQ, DEC_SEQ)


def _rope(x, cos, sin):
    return x * cos + pltpu.roll(x, A_DK // 2, 1) * sin


def _ret_kernel(*refs, chunk, nch, aliased):
    if aliased:
        refs = refs[:-4] + refs[-3:]
    (q_ref, k_ref, v_ref, g_ref, cos_ref, sin_ref, dm_ref, qd_ref, kd_ref, cd_ref, gn_ref, s0_ref,
     y_ref, st_ref, s_scr) = refs
    n = pl.program_id(2)

    @pl.when(n == 0)
    def _():
        s_scr[...] = s0_ref[...]

    dm = dm_ref[...]
    qd = qd_ref[...]
    kd = kd_ref[...]
    cd = cd_ref[...]
    gn = gn_ref[...]
    s = s_scr[...]
    for c in range(nch):
        sl = pl.ds(c * chunk, chunk)
        cos = cos_ref[sl, :]
        sin = sin_ref[sl, :]
        q = _rope(q_ref[sl, :], cos, sin)
        k = _rope(k_ref[sl, :], cos, sin) * (A_DK ** -0.5)
        v = v_ref[sl, :]
        scores = _dot_nt(q, k) * dm
        o = _dot(scores, v) + _dot(q * qd, s)
        s = s * cd + _dot_tn(k * kd, v)
        mu = jnp.mean(o, axis=-1, keepdims=True)
        d = o - mu
        var = jnp.mean(d * d, axis=-1, keepdims=True)
        on = d * lax.rsqrt(var + EPS) * gn
        y_ref[sl, :] = (_silu(g_ref[sl, :]) * on).astype(y_ref.dtype)
    s_scr[...] = s

    @pl.when(n == pl.num_programs(2) - 1)
    def _():
        st_ref[...] = s


def _retention_tables(chunk):
    lg = jnp.log1p(-jnp.exp2(-5.0 - jnp.arange(A_HEADS, dtype=F32)))
    idx = jnp.arange(chunk, dtype=F32)
    rel = idx[:, None] - idx[None, :]
    dm = jnp.where((rel >= 0)[None], jnp.exp(jnp.maximum(rel, 0.0)[None] * lg[:, None, None]), 0.0)
    qd = jnp.exp((idx[None, :] + 1.0) * lg[:, None])
    kd = jnp.exp((chunk - 1.0 - idx)[None, :] * lg[:, None])
    cd = jnp.exp(chunk * lg)
    return (dm, jnp.broadcast_to(qd[:, :, None], (A_HEADS, chunk, A_DK)),
            jnp.broadcast_to(kd[:, :, None], (A_HEADS, chunk, A_DK)),
            jnp.broadcast_to(cd[:, None, None], (A_HEADS, 1, A_DV)))


def _rope_tables(pos0, length):
    half = A_DK // 2
    inv_freq = ROPE_BASE ** (-jnp.arange(half, dtype=F32) / half)
    pos = pos0 + jnp.arange(length)
    ang = pos.astype(F32)[:, None] * inv_freq[None, :]
    cos, sin = jnp.cos(ang), jnp.sin(ang)
    return jnp.concatenate([cos, cos], axis=-1), jnp.concatenate([-sin, sin], axis=-1)


def _retention(seg, p, y_in, layer_j, gn4, s0, pos0):
    t, c = seg.step, seg.chunk
    dm, qd, kd, cd = _retention_tables(c)
    cos, sin = _rope_tables(pos0, seg.length)
    rb = seg.rb
    q_blk = A_QK // A_DK
    v_blk = 2 * A_QK // A_DV
    g_blk = (2 * A_QK + A_V) // A_DV
    in_specs = [
        pl.BlockSpec((t, A_DK), lambda b, h, n: (rb(b, n), h)),
        pl.BlockSpec((t, A_DK), lambda b, h, n: (rb(b, n), q_blk + h)),
        pl.BlockSpec((t, A_DV), lambda b, h, n: (rb(b, n), v_blk + h)),
        pl.BlockSpec((t, A_DV), lambda b, h, n: (rb(b, n), g_blk + h)),
        pl.BlockSpec((t, A_DK), lambda b, h, n: (n, 0)),
        pl.BlockSpec((t, A_DK), lambda b, h, n: (n, 0)),
        pl.BlockSpec((None, c, c), lambda b, h, n: (h, 0, 0)),
        pl.BlockSpec((None, c, A_DK), lambda b, h, n: (h, 0, 0)),
        pl.BlockSpec((None, c, A_DK), lambda b, h, n: (h, 0, 0)),
        pl.BlockSpec((None, 1, A_DV), lambda b, h, n: (h, 0, 0)),
        pl.BlockSpec((None, None, 1, A_DV), lambda b, h, n: (layer_j, h, 0, 0)),
        pl.BlockSpec((None, None, A_DK, A_DV), lambda b, h, n: (b, h, 0, 0)),
    ]
    args = [p, p, p, p, cos, sin, dm, qd, kd, cd, gn4, s0]
    aliases = {}
    if y_in is not None:
        in_specs.append(pl.BlockSpec(memory_space=pl.ANY))
        args.append(y_in)
        aliases = {len(args) - 1: 0}
    return pl.pallas_call(
        functools.partial(_ret_kernel, chunk=c, nch=seg.nch, aliased=y_in is not None),
        out_shape=(jax.ShapeDtypeStruct((M_ALL, D_MODEL), BF16),
                   jax.ShapeDtypeStruct((seg.batch, A_HEADS, A_DK, A_DV), F32)),
        grid=(seg.batch, A_HEADS, seg.nblk),
        in_specs=in_specs,
        out_specs=(pl.BlockSpec((t, A_DV), lambda b, h, n: (rb(b, n), h)),
                   pl.BlockSpec((None, None, A_DK, A_DV), lambda b, h, n: (b, h, 0, 0))),
        scratch_shapes=[pltpu.VMEM((A_DK, A_DV), F32)],
        input_output_aliases=aliases,
        compiler_params=_cparams(3),
        name="retention",
    )(*args)


def _unit_lower_inverse(a, eye, same_blk, chunk):
    d = jnp.where(same_blk, a, 0.0)
    e = a - d
    d2 = _dot3(d, d)
    d4 = _dot3(d2, d2)
    d8 = _dot3(d4, d4)
    p = eye - d
    p = p + _dot3(p, d2)
    p = p + _dot3(p, d4)
    p = p + _dot3(p, d8)
    f = _dot3(p, e)
    x = eye - f
    nblocks = chunk // BD
    power = 2
    fp = f
    while power < nblocks:
        fp = _dot3(fp, fp)
        x = x + _dot3(x, fp)
        power *= 2
    return _dot3(x, p)


def _delta_kernel(*refs, chunk, nch):
    (q_ref, k_ref, v_ref, z_ref, bg_ref, cwq_ref, cwk_ref, cwv_ref, tq_ref, tk_ref, tv_ref, dn_ref, s0_ref,
     _, y_ref, st_ref, s_scr, tail_scr) = refs
    h = pl.program_id(1)
    n = pl.program_id(2)
    t = chunk * nch

    @pl.when(n == 0)
    def _():
        s_scr[...] = s0_ref[...]
        tail_scr[0] = tq_ref[...]
        tail_scr[1] = tk_ref[...]
        tail_scr[2] = tv_ref[...]

    def conv(u_ref, cw_ref, part):
        u = u_ref[...]
        ext = jnp.concatenate([tail_scr[part], u], axis=0)
        w = cw_ref[...]
        out = u * w[CONV_W - 1:CONV_W, :]
        for i in range(CONV_W - 1):
            out = out + ext[8 - (CONV_W - 1) + i:8 - (CONV_W - 1) + i + t, :] * w[i:i + 1, :]
        tail_scr[part] = u[t - 8:t, :]
        return _silu(out)

    cq = conv(q_ref, cwq_ref, 0)
    ck = conv(k_ref, cwk_ref, 1)
    v_all = conv(v_ref, cwv_ref, 2)
    q_all = cq * lax.rsqrt(jnp.sum(cq * cq, axis=-1, keepdims=True) + EPS) * (B_DK ** -0.5)
    k_all = ck * lax.rsqrt(jnp.sum(ck * ck, axis=-1, keepdims=True) + EPS)

    bg = bg_ref[...]
    lane = lax.broadcasted_iota(jnp.int32, bg.shape, 1)
    beta_all = jnp.sum(jnp.where(lane == h, bg, 0.0), axis=-1, keepdims=True)
    g_all = jnp.sum(jnp.where(lane == h + B_HEADS, bg, 0.0), axis=-1, keepdims=True)
    gb_all = jnp.broadcast_to(g_all, (t, LANES))

    ii = lax.broadcasted_iota(jnp.int32, (chunk, chunk), 0)
    jj = lax.broadcasted_iota(jnp.int32, (chunk, chunk), 1)
    lower = ii >= jj
    strict = ii > jj
    same_blk = lax.shift_right_logical(ii, 4) == lax.shift_right_logical(jj, 4)
    ltri = jnp.where(lower, 1.0, 0.0).astype(BF16)
    eye = jnp.where(ii == jj, 1.0, 0.0)
    dn = dn_ref[...]

    s = s_scr[...]
    for c in range(nch):
        lo = c * chunk
        qc = q_all[lo:lo + chunk]
        kc = k_all[lo:lo + chunk]
        vc = v_all[lo:lo + chunk]
        bc = beta_all[lo:lo + chunk]
        gcb = _dot_exact_lhs(ltri, gb_all[lo:lo + chunk])
        col = gcb[:, :chunk]
        row = gcb.T[:chunk, :]
        dmask = jnp.where(lower, jnp.exp(jnp.where(lower, col - row, 0.0)), 0.0)
        kb = kc * bc
        a = jnp.where(strict, _dot_nt(kb, kc) * dmask, 0.0)
        tinv = _unit_lower_inverse(a, eye, same_blk, chunk)
        eg = jnp.exp(gcb)
        uw = _dot3(tinv, jnp.concatenate([vc * bc, kb * eg], axis=1))
        u = uw[:, :B_DV]
        w = uw[:, B_DV:]
        qk = _dot_nt(qc, kc) * dmask
        last = gcb[chunk - 1:chunk, :]
        kdec = kc * jnp.exp(last - gcb)
        r = _dot(jnp.concatenate([w, qc * eg], axis=0), s)
        v_new = u - r[:chunk]
        o = r[chunk:] + _dot(qk, v_new)
        s = s * jnp.exp(last) + _dot_tn(kdec, v_new)
        on = o * lax.rsqrt(jnp.mean(o * o, axis=-1, keepdims=True) + EPS) * dn
        y_ref[lo:lo + chunk, :] = (on * _silu(z_ref[lo:lo + chunk, :])).astype(y_ref.dtype)
    s_scr[...] = s

    @pl.when(n == pl.num_programs(2) - 1)
    def _():
        st_ref[...] = s


def _gated_delta(seg, p, bg, y_in, layer_j, conv_w, tails, dn3, s0):
    t, c = seg.step, seg.chunk
    rb = seg.rb
    q_blk = (2 * A_QK + 2 * A_V) // B_DK
    z_blk = (2 * A_QK + 2 * A_V + CONV_CH) // B_DV
    y_blk = A_V // B_DV

    def pcol(off):
        return pl.BlockSpec((t, B_DK), lambda b, h, n: (rb(b, n), off + h))

    def wcol(part):
        return pl.BlockSpec((None, CONV_W, B_DK), lambda b, h, n: (layer_j, 0, part * B_HEADS + h))

    def tcol(part):
        return pl.BlockSpec((None, 8, B_DK), lambda b, h, n: (b, 0, part * B_HEADS + h))

    in_specs = [
        pcol(q_blk), pcol(q_blk + B_HEADS), pcol(q_blk + 2 * B_HEADS), pcol(z_blk),
        pl.BlockSpec((t, LANES), lambda b, h, n: (rb(b, n), 0)),
        wcol(0), wcol(1), wcol(2), tcol(0), tcol(1), tcol(2),
        pl.BlockSpec((None, 1, B_DV), lambda b, h, n: (layer_j, 0, 0)),
        pl.BlockSpec((None, None, B_DK, B_DV), lambda b, h, n: (b, h, 0, 0)),
        pl.BlockSpec(memory_space=pl.ANY),
    ]
    return pl.pallas_call(
        functools.partial(_delta_kernel, chunk=c, nch=seg.nch),
        out_shape=(jax.ShapeDtypeStruct((M_ALL, D_MODEL), BF16),
                   jax.ShapeDtypeStruct((seg.batch, B_HEADS, B_DK, B_DV), F32)),
        grid=(seg.batch, B_HEADS, seg.nblk),
        in_specs=in_specs,
        out_specs=(pl.BlockSpec((t, B_DV), lambda b, h, n: (rb(b, n), y_blk + h)),
                   pl.BlockSpec((None, None, B_DK, B_DV), lambda b, h, n: (b, h, 0, 0))),
        scratch_shapes=[pltpu.VMEM((B_DK, B_DV), F32), pltpu.VMEM((3, 8, B_DK), F32)],
        input_output_aliases={13: 0},
        compiler_params=_cparams(3),
        name="gated_delta",
    )(p, p, p, p, bg, conv_w, conv_w, conv_w, tails, tails, tails, dn3, s0, y_in)


TQ = 256
BAND = BAND_PREV * CHUNK
WIN = BAND + TQ
NEG = -1e30


def _softmax_pv(s, v):
    m = jnp.max(s, axis=-1, keepdims=True)
    p = jnp.exp(s - m)
    l = jnp.sum(p, axis=-1, keepdims=True)
    return _dot(p, v) / l


def _attn_prompt_kernel(q_ref, k_ref, v_ref, bias_ref, o_ref, kp_scr, vp_scr):
    n = pl.program_id(2)

    @pl.when(n == 0)
    def _():
        kp_scr[0:BAND, :] = jnp.zeros((BAND, C_HD), BF16)
        vp_scr[0:BAND, :] = jnp.zeros((BAND, C_HD), BF16)
        kp_scr[BAND:, :] = k_ref[...].astype(BF16)
        vp_scr[BAND:, :] = v_ref[...].astype(BF16)

    start = pl.multiple_of(n * TQ, TQ)
    kw = kp_scr[pl.ds(start, WIN), :]
    vw = vp_scr[pl.ds(start, WIN), :]
    s = _dot_nt(q_ref[...], kw) * (C_HD ** -0.5) + bias_ref[...]
    col = lax.broadcasted_iota(jnp.int32, s.shape, 1)
    s = jnp.where(col + start >= BAND, s, NEG)
    o_ref[...] = _softmax_pv(s, vw).astype(o_ref.dtype)


def _prompt_bias(rel_bias):
    r = jnp.arange(TQ)
    w = jnp.arange(WIN)
    j = w[None, :] - (r[:, None] // CHUNK) * CHUNK
    inside = (j >= 0) & (j < BAND + CHUNK)
    rel = (r[:, None] % CHUNK) + BAND - j
    tbl = rel_bias[jnp.clip(rel, -REL_CLIP, REL_CLIP) + REL_CLIP]
    return jnp.where(inside[None], jnp.transpose(tbl, (2, 0, 1)), NEG)


def _attention_prompt(p, rel_bias):
    nq = SEQ // TQ
    k_blk = C_WIDTH // C_HD
    return pl.pallas_call(
        _attn_prompt_kernel,
        out_shape=jax.ShapeDtypeStruct((M_ALL, C_WIDTH), BF16),
        grid=(BATCH, C_HEADS, nq),
        in_specs=[
            pl.BlockSpec((TQ, C_HD), lambda b, h, n: (b * nq + n, h)),
            pl.BlockSpec((SEQ, C_HD), lambda b, h, n: (b, k_blk + h)),
            pl.BlockSpec((SEQ, C_HD), lambda b, h, n: (b, 2 * k_blk + h)),
            pl.BlockSpec((None, TQ, WIN), lambda b, h, n: (h, 0, 0)),
        ],
        out_specs=pl.BlockSpec((TQ, C_HD), lambda b, h, n: (b * nq + n, h)),
        scratch_shapes=[pltpu.VMEM((BAND + SEQ, C_HD), BF16), pltpu.VMEM((BAND + SEQ, C_HD), BF16)],
        compiler_params=_cparams(3),
        name="band_attention_prompt",
    )(p, p, p, _prompt_bias(rel_bias))


HB = 8


def _attn_sample_kernel(q_ref, kn_ref, vn_ref, ck_ref, cv_ref, bias_ref, _, o_ref):
    for hh in range(HB):
        cs = slice(hh * C_HD, (hh + 1) * C_HD)
        kk = jnp.concatenate([ck_ref[:, cs], kn_ref[:, cs]], axis=0)
        vv = jnp.concatenate([cv_ref[:, cs], vn_ref[:, cs]], axis=0)
        s = _dot_nt(q_ref[:, cs], kk) * (C_HD ** -0.5) + bias_ref[hh]
        o_ref[:, cs] = _softmax_pv(s, vv).astype(o_ref.dtype)


def _sample_bias(rel_bias, lc):
    t = jnp.arange(DEC_SEQ)
    j = jnp.arange(lc + DEC_SEQ)
    rel = t[:, None] + lc - j[None, :]
    return jnp.transpose(rel_bias[jnp.clip(rel, -REL_CLIP, REL_CLIP) + REL_CLIP], (2, 0, 1))


def _attention_sample(p, o_in, cache_k, cache_v, rel_bias):
    lc = cache_k.shape[1]
    rb0 = M_PROMPT // DEC_SEQ
    wide = HB * C_HD
    k_blk = C_WIDTH // wide
    ck = cache_k.reshape(DEC_BATCH, lc, C_WIDTH)
    cv = cache_v.reshape(DEC_BATCH, lc, C_WIDTH)
    return pl.pallas_call(
        _attn_sample_kernel,
        out_shape=jax.ShapeDtypeStruct((M_ALL, C_WIDTH), BF16),
        grid=(DEC_BATCH, C_HEADS // HB),
        in_specs=[
            pl.BlockSpec((DEC_SEQ, wide), lambda b, g: (rb0 + b, g)),
            pl.BlockSpec((DEC_SEQ, wide), lambda b, g: (rb0 + b, k_blk + g)),
            pl.BlockSpec((DEC_SEQ, wide), lambda b, g: (rb0 + b, 2 * k_blk + g)),
            pl.BlockSpec((None, lc, wide), lambda b, g: (b, 0, g)),
            pl.BlockSpec((None, lc, wide), lambda b, g: (b, 0, g)),
            pl.BlockSpec((HB, DEC_SEQ, lc + DEC_SEQ), lambda b, g: (g, 0, 0)),
            pl.BlockSpec(memory_space=pl.ANY),
        ],
        out_specs=pl.BlockSpec((DEC_SEQ, wide), lambda b, g: (rb0 + b, g)),
        input_output_aliases={6: 0},
        compiler_params=_cparams(2),
        name="band_attention_sample",
    )(p, p, p, ck, cv, _sample_bias(rel_bias, lc), o_in)


def _conv_tail(state_conv):
    return jnp.pad(state_conv, ((0, 0), (8 - (CONV_W - 1), 0), (0, 0)))


def kernel(x_prompt, x_sample, c_prompt, c_sample, state_ret_0, state_delta_0, state_conv_0, cache_k_1, cache_v_1, state_ret_2, state_delta_2, state_conv_2, cache_k_3, cache_v_3, norm_mix_g, norm_ffn_g, w_mod, b_mod, w_in_even, w_out_even, a_norm_g, b_conv_w, b_a_log, b_dt_bias, b_norm_g, w_in_odd, w_out_odd, c_rel_bias, w_ff1, w_ff2, final_norm_g):
    x3 = jnp.concatenate([x_prompt.reshape(M_PROMPT // GROUP, GROUP, D_MODEL),
                          x_sample.reshape(M_SAMPLE // GROUP, GROUP, D_MODEL)], axis=0)
    c_all = jnp.concatenate([c_prompt, c_sample, jnp.zeros((N_SEQ_PAD - N_SEQ, D_MODEL), F32)], axis=0)
    mod = _modulation(c_all, w_mod, b_mod)

    sample_states = {0: (state_ret_0, state_delta_0, state_conv_0), 1: (cache_k_1, cache_v_1),
                     2: (state_ret_2, state_delta_2, state_conv_2), 3: (cache_k_3, cache_v_3)}
    gn4 = a_norm_g.reshape(-1, A_HEADS, 1, A_DV)
    dn3 = b_norm_g.reshape(-1, 1, B_DV)
    zeros_ret = jnp.zeros((BATCH, A_HEADS, A_DK, A_DV), F32)
    zeros_delta = jnp.zeros((BATCH, B_HEADS, B_DK, B_DV), F32)
    zeros_tail = jnp.zeros((BATCH, 8, CONV_CH), F32)

    p_states, s_states = [], []
    for l in range(DEPTH):
        j = l // 2
        gm = jnp.concatenate([jnp.repeat(mod[l, :BATCH], SEQ // GROUP, axis=0), mod[l, BATCH:N_SEQ]], axis=0)[:, None, :]
        h = _norm_modulate(x3, norm_mix_g[l][None, :], gm, 0, 1)
        if l % 2 == 0:
            p = _matmul(h, w_in_even, j, EVEN_MAIN, relu2=False, out_dtype=F32)
            w_small = jnp.pad(w_in_even[j, :, EVEN_MAIN:], ((0, 0), (0, LANES - 2 * B_HEADS))).astype(BF16)
            alog_row = jnp.pad(b_a_log[j], (B_HEADS, LANES - 2 * B_HEADS))[None, :]
            dtb_row = jnp.pad(b_dt_bias[j], (B_HEADS, LANES - 2 * B_HEADS))[None, :]
            bg = _beta_gate(h, w_small, alog_row, dtb_row)
            s_ret, s_delta, s_conv = sample_states[l]
            y, p_ret = _retention(SEG_PROMPT, p, None, j, gn4, zeros_ret, 0)
            y, s_ret_new = _retention(SEG_SAMPLE, p, y, j, gn4, s_ret, PAST_LEN)
            y, p_delta = _gated_delta(SEG_PROMPT, p, bg, y, j, b_conv_w, zeros_tail, dn3, zeros_delta)
            y, s_delta_new = _gated_delta(SEG_SAMPLE, p, bg, y, j, b_conv_w, _conv_tail(s_conv), dn3, s_delta)
            c0 = 2 * A_QK + 2 * A_V
            p_conv = p[:M_PROMPT, c0:c0 + CONV_CH].reshape(BATCH, SEQ, CONV_CH)[:, SEQ - (CONV_W - 1):]
            s_conv_new = p[M_PROMPT:, c0:c0 + CONV_CH].reshape(DEC_BATCH, DEC_SEQ, CONV_CH)[:, DEC_SEQ - (CONV_W - 1):]
            p_states.append((p_ret, p_delta, p_conv))
            s_states.append((s_ret_new, s_delta_new, s_conv_new))
            x3 = _matmul_residual(y, w_out_even, j, x3, gm, 2)
        else:
            p = _matmul(h, w_in_odd, j, 3 * C_WIDTH, relu2=False, out_dtype=F32)
            cache_k, cache_v = sample_states[l]
            o = _attention_prompt(p, c_rel_bias[j])
            o = _attention_sample(p, o, cache_k, cache_v, c_rel_bias[j])
            keep = min(BAND, SEQ)
            pk = p[:M_PROMPT, C_WIDTH:2 * C_WIDTH].reshape(BATCH, SEQ, C_HEADS, C_HD)[:, SEQ - keep:]
            pv = p[:M_PROMPT, 2 * C_WIDTH:].reshape(BATCH, SEQ, C_HEADS, C_HD)[:, SEQ - keep:]
            sk = p[M_PROMPT:, C_WIDTH:2 * C_WIDTH].reshape(DEC_BATCH, DEC_SEQ, C_HEADS, C_HD)
            sv = p[M_PROMPT:, 2 * C_WIDTH:].reshape(DEC_BATCH, DEC_SEQ, C_HEADS, C_HD)
            p_states.append((pk, pv))
            s_states.append((sk, sv))
            x3 = _matmul_residual(o, w_out_odd, j, x3, gm, 2)
        h = _norm_modulate(x3, norm_ffn_g[l][None, :], gm, 3, 4)
        f = _matmul(h, w_ff1, l, D_FF, relu2=True, out_dtype=BF16)
        for kblk in range(D_FF // D_MODEL):
            x3 = _matmul_residual(f, w_ff2, l, x3, gm, 5, kblk=kblk)

    g_row = final_norm_g[None, :]
    y_prompt = _final_norm(x3, g_row, 0, M_PROMPT // GROUP).reshape(BATCH, SEQ, D_MODEL)
    y_sample = _final_norm(x3, g_row, M_PROMPT // GROUP, M_SAMPLE // GROUP).reshape(DEC_BATCH, DEC_SEQ, D_MODEL)
    (p_ret_0, p_delta_0, p_conv_0), (p_k_1, p_v_1), (p_ret_2, p_delta_2, p_conv_2), (p_k_3, p_v_3) = p_states
    (s_ret_0, s_delta_0, s_conv_0), (s_k_1, s_v_1), (s_ret_2, s_delta_2, s_conv_2), (s_k_3, s_v_3) = s_states
    return (y_prompt, y_sample,
            p_ret_0, s_ret_0, p_delta_0, s_delta_0, p_conv_0, s_conv_0,
            p_k_1, s_k_1, p_v_1, s_v_1,
            p_ret_2, s_ret_2, p_delta_2, s_delta_2, p_conv_2, s_conv_2,
            p_k_3, s_k_3, p_v_3, s_v_3)
```

```python
import functools

import numpy as np
import jax
import jax.numpy as jnp
from jax import lax
from jax.experimental import pallas as pl
from jax.experimental.pallas import tpu as pltpu

F32 = jnp.float32
BF16 = jnp.bfloat16

D_MODEL = 4096
BATCH = 4
SEQ = 4096
DEPTH = 4
DEC_BATCH = 32
DEC_SEQ = 32
PAST_LEN = 1024
CHUNK = 64
EPS = 1e-6
A_DK = 128
A_DV = 256
A_HEADS = 8
A_QK = A_HEADS * A_DK
A_V = A_HEADS * A_DV
ROPE_BASE = 10000.0
B_DK = 128
B_DV = 128
B_HEADS = 16
B_QK = B_HEADS * B_DK
B_V = B_HEADS * B_DV
CONV_W = 4
CONV_CH = 2 * B_QK + B_V
EVEN_MAIN = 2 * A_QK + 2 * A_V + CONV_CH + B_V
EVEN_IN = EVEN_MAIN + 2 * B_HEADS
C_HD = 128
C_HEADS = 32
C_WIDTH = C_HEADS * C_HD
BAND_PREV = 8
REL_CLIP = 128
D_FF = 4 * D_MODEL

GROUP = 32
M_PROMPT = BATCH * SEQ
M_SAMPLE = DEC_BATCH * DEC_SEQ
M_ALL = M_PROMPT + M_SAMPLE
N_SEQ = BATCH + DEC_BATCH
N_SEQ_PAD = 40

VMEM_LIMIT = 56 * 1024 * 1024
TM = 1024
TN = 512
LANES = 128
BD = 16


def _cparams(n_axes):
    return pltpu.CompilerParams(dimension_semantics=("arbitrary",) * n_axes, vmem_limit_bytes=VMEM_LIMIT)


def _sigmoid(x):
    return 1.0 / (1.0 + jnp.exp(-x))


def _silu(x):
    return x * _sigmoid(x)


def _dot(a, b):
    return jnp.dot(a.astype(BF16), b.astype(BF16), preferred_element_type=F32)


def _dot_nt(a, b):
    return lax.dot_general(a.astype(BF16), b.astype(BF16), (((1,), (1,)), ((), ())), preferred_element_type=F32)


def _dot_tn(a, b):
    return lax.dot_general(a.astype(BF16), b.astype(BF16), (((0,), (0,)), ((), ())), preferred_element_type=F32)


def _dot_exact_lhs(l_bf16, x):
    d = functools.partial(jnp.dot, preferred_element_type=F32)
    x0 = x.astype(BF16)
    r1 = x - x0.astype(F32)
    x1 = r1.astype(BF16)
    x2 = (r1 - x1.astype(F32)).astype(BF16)
    return d(l_bf16, x0) + (d(l_bf16, x1) + d(l_bf16, x2))


def _mod_kernel(c_ref, w_ref, b_ref, o_ref):
    c = c_ref[...]
    o_ref[...] = _dot(_silu(c), w_ref[...]) + b_ref[...]


def _modulation(c_all, w_mod, b_mod):
    n_out = 6 * D_MODEL
    return pl.pallas_call(
        _mod_kernel,
        out_shape=jax.ShapeDtypeStruct((DEPTH, N_SEQ_PAD, n_out), F32),
        grid=(DEPTH, n_out // TN),
        in_specs=[
            pl.BlockSpec((N_SEQ_PAD, D_MODEL), lambda l, j: (0, 0)),
            pl.BlockSpec((None, D_MODEL, TN), lambda l, j: (l, 0, j)),
            pl.BlockSpec((None, 1, TN), lambda l, j: (l, 0, j)),
        ],
        out_specs=pl.BlockSpec((None, N_SEQ_PAD, TN), lambda l, j: (l, 0, j)),
        compiler_params=_cparams(2),
        name="modulation",
    )(c_all, w_mod, b_mod.reshape(DEPTH, 1, n_out))


def _prompt_mod_spec(width, layer, col, rows_per_tile, tile_of):
    tiles_per_seq = SEQ // rows_per_tile
    return pl.BlockSpec(
        (None, None, 1, width),
        lambda *g: (layer, DEC_BATCH + jnp.minimum(tile_of(*g) // tiles_per_seq, BATCH - 1), 0, col(*g)))


def _sample_mod_spec(width, layer, col, rows_per_tile, tile_of):
    seqs = rows_per_tile // GROUP
    first = M_PROMPT // rows_per_tile
    return pl.BlockSpec(
        (None, seqs, 1, width),
        lambda *g: (layer, jnp.maximum(tile_of(*g) - first, 0), 0, col(*g)))


def _tile_mod(is_sample, gp_ref, gs_ref):
    return jnp.where(is_sample, gs_ref[...], gp_ref[...][None])


NM_ROWS = 256


def _normmod_kernel(x_ref, g_ref, shp_ref, shs_ref, scp_ref, scs_ref, o_ref):
    is_sample = pl.program_id(0) >= M_PROMPT // NM_ROWS
    x = x_ref[...]
    y = x * lax.rsqrt(jnp.mean(x * x, axis=-1, keepdims=True) + EPS) * g_ref[...]
    y3 = y.reshape(NM_ROWS // GROUP, GROUP, D_MODEL)
    h = y3 * (1.0 + _tile_mod(is_sample, scp_ref, scs_ref)) + _tile_mod(is_sample, shp_ref, shs_ref)
    o_ref[...] = h.reshape(NM_ROWS, D_MODEL).astype(o_ref.dtype)


def _norm_modulate(x, g_row, mod4, layer, shift_col, scale_col):
    def specs(c):
        args = (D_MODEL, layer, lambda i: c, NM_ROWS, lambda i: i)
        return [_prompt_mod_spec(*args), _sample_mod_spec(*args)]

    return pl.pallas_call(
        _normmod_kernel,
        out_shape=jax.ShapeDtypeStruct((M_ALL, D_MODEL), BF16),
        grid=(M_ALL // NM_ROWS,),
        in_specs=[
            pl.BlockSpec((NM_ROWS, D_MODEL), lambda i: (i, 0)),
            pl.BlockSpec((1, D_MODEL), lambda i: (0, 0)),
        ] + specs(shift_col) + specs(scale_col),
        out_specs=pl.BlockSpec((NM_ROWS, D_MODEL), lambda i: (i, 0)),
        compiler_params=_cparams(1),
        name="norm_modulate",
    )(x, g_row, mod4, mod4, mod4, mod4)


def _final_norm_kernel(x_ref, g_ref, o_ref):
    x = x_ref[...]
    o_ref[...] = x * lax.rsqrt(jnp.mean(x * x, axis=-1, keepdims=True) + EPS) * g_ref[...]


def _final_norm(x, g_row, row0, n_rows):
    return pl.pallas_call(
        _final_norm_kernel,
        out_shape=jax.ShapeDtypeStruct((n_rows, D_MODEL), F32),
        grid=(n_rows // NM_ROWS,),
        in_specs=[
            pl.BlockSpec((NM_ROWS, D_MODEL), lambda i: (row0 // NM_ROWS + i, 0)),
            pl.BlockSpec((1, D_MODEL), lambda i: (0, 0)),
        ],
        out_specs=pl.BlockSpec((NM_ROWS, D_MODEL), lambda i: (i, 0)),
        compiler_params=_cparams(1),
        name="final_norm",
    )(x, g_row)


def _cast_weight(w_ref, wb_ref):
    @pl.when(pl.program_id(1) == 0)
    def _():
        wb_ref[...] = w_ref[...].astype(BF16)


def _mm_kernel(a_ref, w_ref, o_ref, wb_ref, *, relu2):
    _cast_weight(w_ref, wb_ref)
    acc = jnp.dot(a_ref[...], wb_ref[...], preferred_element_type=F32)
    if relu2:
        acc = jnp.square(jnp.maximum(acc, 0.0))
    o_ref[...] = acc.astype(o_ref.dtype)


def _matmul(a, w, layer, n_cols, *, relu2, out_dtype):
    k = a.shape[1]
    return pl.pallas_call(
        functools.partial(_mm_kernel, relu2=relu2),
        out_shape=jax.ShapeDtypeStruct((M_ALL, n_cols), out_dtype),
        grid=(n_cols // TN, M_ALL // TM),
        in_specs=[
            pl.BlockSpec((TM, k), lambda j, i: (i, 0)),
            pl.BlockSpec((None, k, TN), lambda j, i: (layer, 0, j)),
        ],
        out_specs=pl.BlockSpec((TM, TN), lambda j, i: (i, j)),
        scratch_shapes=[pltpu.VMEM((k, TN), BF16)],
        compiler_params=_cparams(2),
        name="proj_relu2" if relu2 else "proj",
    )(a, w)


def _mm_res_kernel(a_ref, w_ref, x_ref, gp_ref, gs_ref, o_ref, wb_ref):
    _cast_weight(w_ref, wb_ref)
    acc = jnp.dot(a_ref[...], wb_ref[...], preferred_element_type=F32)
    gate = _tile_mod(pl.program_id(1) >= M_PROMPT // TM, gp_ref, gs_ref)
    shape3 = (TM // GROUP, GROUP, TN)
    o_ref[...] = (x_ref[...].reshape(shape3) + gate * acc.reshape(shape3)).reshape(TM, TN)


def _matmul_residual(a, w, w_layer, x, mod4, layer, gate_col, *, kblk=0, kc=D_MODEL):
    nj = D_MODEL // TN
    mod_args = (TN, layer, lambda j, i: gate_col * nj + j, TM, lambda j, i: i)
    return pl.pallas_call(
        _mm_res_kernel,
        out_shape=jax.ShapeDtypeStruct(x.shape, F32),
        grid=(nj, M_ALL // TM),
        in_specs=[
            pl.BlockSpec((TM, kc), lambda j, i: (i, kblk)),
            pl.BlockSpec((None, kc, TN), lambda j, i: (w_layer, kblk, j)),
            pl.BlockSpec((TM, TN), lambda j, i: (i, j)),
            _prompt_mod_spec(*mod_args),
            _sample_mod_spec(*mod_args),
        ],
        out_specs=pl.BlockSpec((TM, TN), lambda j, i: (i, j)),
        scratch_shapes=[pltpu.VMEM((kc, TN), BF16)],
        input_output_aliases={2: 0},
        compiler_params=_cparams(2),
        name="proj_residual",
    )(a, w, x, mod4, mod4)


CS_ROWS = 256


def _bg_kernel(a_ref, w_ref, alog_ref, dtb_ref, o_ref):
    acc = jnp.dot(a_ref[...], w_ref[...], preferred_element_type=F32)
    lane = lax.broadcasted_iota(jnp.int32, (CS_ROWS, LANES), 1)
    z = acc + dtb_ref[...]
    softplus = jnp.maximum(z, 0.0) + jnp.log1p(jnp.exp(-jnp.abs(z)))
    g = -jnp.exp(alog_ref[...]) * softplus
    is_sample = pl.program_id(0) * TM >= M_PROMPT
    ii = lax.broadcasted_iota(jnp.int32, (CS_ROWS, CS_ROWS), 0)
    jj = lax.broadcasted_iota(jnp.int32, (CS_ROWS, CS_ROWS), 1)
    chunk_i = jnp.where(is_sample, lax.shift_right_logical(ii, 5), lax.shift_right_logical(ii, 6))
    chunk_j = jnp.where(is_sample, lax.shift_right_logical(jj, 5), lax.shift_right_logical(jj, 6))
    ltri = jnp.where(chunk_i == chunk_j, jnp.where(ii >= jj, 1.0, 0.0), 0.0).astype(BF16)
    for r in range(TM // CS_ROWS):
        rows = slice(r * CS_ROWS, (r + 1) * CS_ROWS)
        gc = _dot_exact_lhs(ltri, g[rows])
        o_ref[rows, :] = jnp.where(lane < B_HEADS, _sigmoid(acc[rows]),
                                   jnp.where(lane < 2 * B_HEADS, g[rows], gc))


def _beta_gate(a, w_small, alog_row, dtb_row):
    return pl.pallas_call(
        _bg_kernel,
        out_shape=jax.ShapeDtypeStruct((M_ALL, LANES), F32),
        grid=(M_ALL // TM,),
        in_specs=[
            pl.BlockSpec((TM, D_MODEL), lambda i: (i, 0)),
            pl.BlockSpec((D_MODEL, LANES), lambda i: (0, 0)),
            pl.BlockSpec((1, LANES), lambda i: (0, 0)),
            pl.BlockSpec((1, LANES), lambda i: (0, 0)),
        ],
        out_specs=pl.BlockSpec((TM, LANES), lambda i: (i, 0)),
        compiler_params=_cparams(1),
        name="beta_gate",
    )(a, w_small, alog_row, dtb_row)


class _Seg:
    def __init__(self, batch, length, row0, step, chunk, pos0):
        self.batch, self.length, self.row0, self.step, self.chunk, self.pos0 = batch, length, row0, step, chunk, pos0
        self.nblk = length // step
        self.nch = step // chunk

    def rb(self, b, n):
        return (self.row0 + b * self.length) // self.step + n


SEG_PROMPT = _Seg(BATCH, SEQ, 0, 256, CHUNK, 0)
SEG_SAMPLE = _Seg(DEC_BATCH, DEC_SEQ, M_PROMPT, DEC_SEQ, DEC_SEQ, PAST_LEN)


def _chunk_ids(t, chunk, width):
    return lax.shift_right_logical(lax.broadcasted_iota(jnp.int32, (t, width), 0), chunk.bit_length() - 1)


HPA = 2


def _rope(x, cos, sin):
    return x * cos + pltpu.roll(x, A_DK // 2, 1) * sin


def _ret_kernel(*refs, chunk, nch, aliased):
    if aliased:
        refs = refs[:-4] + refs[-3:]
    (q_ref, k_ref, v_ref, g_ref, cos_ref, sin_ref, dm_ref, qd_ref, kd_ref, cd_ref, gn_ref, s0_ref,
     y_ref, st_ref, s_scr) = refs
    n = pl.program_id(2)
    t = chunk * nch

    @pl.when(n == 0)
    def _():
        s_scr[...] = s0_ref[...]

    cos = cos_ref[...]
    sin = sin_ref[...]
    cid_k = _chunk_ids(t, chunk, A_DK)
    cid_v = _chunk_ids(t, chunk, A_DV)
    for hh in range(HPA):
        ks = slice(hh * A_DK, (hh + 1) * A_DK)
        vs = slice(hh * A_DV, (hh + 1) * A_DV)
        q = _rope(q_ref[:, ks], cos, sin)
        k = _rope(k_ref[:, ks], cos, sin) * (A_DK ** -0.5)
        v = v_ref[:, vs]
        scores = _dot_nt(q, k) * dm_ref[hh]
        kdec = k * kd_ref[hh]
        if nch > 1:
            vcat = jnp.concatenate([jnp.where(cid_v == c, v, 0.0) for c in range(nch)], axis=1)
        else:
            vcat = v
        kv = _dot_tn(kdec, vcat)
        cd = cd_ref[hh]
        states = [s_scr[hh]]
        for c in range(nch):
            states.append(states[-1] * cd + kv[:, c * A_DV:(c + 1) * A_DV])
        s_scr[hh] = states[nch]
        qdec = q * qd_ref[hh]
        if nch > 1:
            qcat = jnp.concatenate([jnp.where(cid_k == c, qdec, 0.0) for c in range(nch)], axis=1)
        else:
            qcat = qdec
        o = _dot(jnp.concatenate([qcat, scores], axis=1), jnp.concatenate(states[:nch] + [v], axis=0))
        mu = jnp.mean(o, axis=-1, keepdims=True)
        d = o - mu
        var = jnp.mean(d * d, axis=-1, keepdims=True)
        on = d * lax.rsqrt(var + EPS) * gn_ref[hh]
        y_ref[:, vs] = (_silu(g_ref[:, vs]) * on).astype(y_ref.dtype)

    @pl.when(n == pl.num_programs(2) - 1)
    def _():
        st_ref[...] = s_scr[...]


def _retention_tables(chunk, nch):
    t = chunk * nch
    lg = np.log1p(-np.exp2(-5.0 - np.arange(A_HEADS, dtype=np.float64)))
    idx = np.arange(t)
    pos = idx % chunk
    rel = pos[:, None] - pos[None, :]
    same = (idx[:, None] // chunk) == (idx[None, :] // chunk)
    dm = np.where((same & (rel >= 0))[None], np.exp(np.maximum(rel, 0)[None] * lg[:, None, None]), 0.0)
    qd = np.exp((pos[None, :] + 1.0) * lg[:, None])
    kd = np.exp((chunk - 1.0 - pos)[None, :] * lg[:, None])
    cd = np.exp(chunk * lg)
    bc = lambda x, w: jnp.asarray(np.broadcast_to(x[:, :, None], x.shape + (w,)), F32)
    return jnp.asarray(dm, F32), bc(qd, A_DK), bc(kd, A_DK), bc(cd[:, None], A_DV)


def _rope_tables(pos0, length):
    half = A_DK // 2
    inv_freq = ROPE_BASE ** (-np.arange(half, dtype=np.float64) / half)
    ang = (pos0 + np.arange(length, dtype=np.float64))[:, None] * inv_freq[None, :]
    cos, sin = np.cos(ang), np.sin(ang)
    return (jnp.asarray(np.concatenate([cos, cos], axis=-1), F32),
            jnp.asarray(np.concatenate([-sin, sin], axis=-1), F32))


def _retention(seg, p, y_in, layer_j, gn4, s0):
    t, c = seg.step, seg.chunk
    dm, qd, kd, cd = _retention_tables(c, seg.nch)
    cos, sin = _rope_tables(seg.pos0, seg.length)
    rb = seg.rb
    kw, vw = HPA * A_DK, HPA * A_DV
    hgroups = A_HEADS // HPA
    k_blk = A_QK // kw
    v_blk = 2 * A_QK // vw
    g_blk = (2 * A_QK + A_V) // vw
    in_specs = [
        pl.BlockSpec((t, kw), lambda b, h, n: (rb(b, n), h)),
        pl.BlockSpec((t, kw), lambda b, h, n: (rb(b, n), k_blk + h)),
        pl.BlockSpec((t, vw), lambda b, h, n: (rb(b, n), v_blk + h)),
        pl.BlockSpec((t, vw), lambda b, h, n: (rb(b, n), g_blk + h)),
        pl.BlockSpec((t, A_DK), lambda b, h, n: (n, 0)),
        pl.BlockSpec((t, A_DK), lambda b, h, n: (n, 0)),
        pl.BlockSpec((HPA, t, t), lambda b, h, n: (h, 0, 0)),
        pl.BlockSpec((HPA, t, A_DK), lambda b, h, n: (h, 0, 0)),
        pl.BlockSpec((HPA, t, A_DK), lambda b, h, n: (h, 0, 0)),
        pl.BlockSpec((HPA, 1, A_DV), lambda b, h, n: (h, 0, 0)),
        pl.BlockSpec((None, HPA, 1, A_DV), lambda b, h, n: (layer_j, h, 0, 0)),
        pl.BlockSpec((None, HPA, A_DK, A_DV), lambda b, h, n: (b, h, 0, 0)),
    ]
    args = [p, p, p, p, cos, sin, dm, qd, kd, cd, gn4, s0]
    aliases = {}
    if y_in is not None:
        in_specs.append(pl.BlockSpec(memory_space=pl.ANY))
        args.append(y_in)
        aliases = {len(args) - 1: 0}
    return pl.pallas_call(
        functools.partial(_ret_kernel, chunk=c, nch=seg.nch, aliased=y_in is not None),
        out_shape=(jax.ShapeDtypeStruct((M_ALL, D_MODEL), BF16),
                   jax.ShapeDtypeStruct((seg.batch, A_HEADS, A_DK, A_DV), F32)),
        grid=(seg.batch, hgroups, seg.nblk),
        in_specs=in_specs,
        out_specs=(pl.BlockSpec((t, vw), lambda b, h, n: (rb(b, n), h)),
                   pl.BlockSpec((None, HPA, A_DK, A_DV), lambda b, h, n: (b, h, 0, 0))),
        scratch_shapes=[pltpu.VMEM((HPA, A_DK, A_DV), F32)],
        input_output_aliases=aliases,
        compiler_params=_cparams(3),
        name="retention",
    )(*args)


HPS = 4


def _unit_lower_solve(a, rhs, same_blk, nblocks):
    n = a.shape[0]
    d = jnp.where(same_blk, a, 0.0)
    y = jnp.concatenate([a - d, rhs], axis=1)
    r = _dot(d, jnp.concatenate([d, y], axis=1))
    d2, y = r[:, :n], y - r[:, n:]
    r = _dot(d2, jnp.concatenate([d2, y], axis=1))
    d4, y = r[:, :n], y + r[:, n:]
    r = _dot(d4, jnp.concatenate([d4, y], axis=1))
    d8, y = r[:, :n], y + r[:, n:]
    y = y + _dot(d8, y)
    f, y = y[:, :n], y[:, n:]
    if nblocks == 1:
        return y
    if nblocks == 2:
        return y - _dot(f, y)
    r = _dot(f, jnp.concatenate([f, y], axis=1))
    f2, y = r[:, :n], y - r[:, n:]
    return y + _dot(f2, y)


def _delta_kernel(*refs, chunk, nch):
    (q_ref, k_ref, v_ref, z_ref, bg_ref, cwq_ref, cwk_ref, cwv_ref, tq_ref, tk_ref, tv_ref, dn_ref, s0_ref,
     _, y_ref, st_ref, s_scr, tail_scr) = refs
    hg = pl.program_id(1)
    n = pl.program_id(2)
    t = chunk * nch
    log2_chunk = chunk.bit_length() - 1

    @pl.when(n == 0)
    def _():
        s_scr[...] = s0_ref[...]
        tail_scr[0] = tq_ref[...]
        tail_scr[1] = tk_ref[...]
        tail_scr[2] = tv_ref[...]

    ii = lax.broadcasted_iota(jnp.int32, (t, t), 0)
    jj = lax.broadcasted_iota(jnp.int32, (t, t), 1)
    same_chunk = lax.shift_right_logical(ii, log2_chunk) == lax.shift_right_logical(jj, log2_chunk)
    lower = jnp.logical_and(same_chunk, ii >= jj)
    strict = jnp.logical_and(same_chunk, ii > jj)
    same_blk = lax.shift_right_logical(ii, BD.bit_length() - 1) == lax.shift_right_logical(jj, BD.bit_length() - 1)
    chunk_id = _chunk_ids(t, chunk, LANES)
    bg = bg_ref[...]
    lane = lax.broadcasted_iota(jnp.int32, bg.shape, 1)
    dn = dn_ref[...]

    for hh in range(HPS):
        cs = slice(hh * B_DK, (hh + 1) * B_DK)
        h = hg * HPS + hh

        def conv(u_ref, cw_ref, part):
            u = u_ref[:, cs]
            ext = jnp.concatenate([tail_scr[part, :, cs], u], axis=0)
            w = cw_ref[:, cs]
            out = u * w[CONV_W - 1:CONV_W, :]
            for i in range(CONV_W - 1):
                out = out + ext[8 - (CONV_W - 1) + i:8 - (CONV_W - 1) + i + t, :] * w[i:i + 1, :]
            tail_scr[part, :, cs] = u[t - 8:t, :]
            return _silu(out)

        cq = conv(q_ref, cwq_ref, 0)
        ck = conv(k_ref, cwk_ref, 1)
        v = conv(v_ref, cwv_ref, 2)
        q = cq * lax.rsqrt(jnp.sum(cq * cq, axis=-1, keepdims=True) + EPS) * (B_DK ** -0.5)
        k = ck * lax.rsqrt(jnp.sum(ck * ck, axis=-1, keepdims=True) + EPS)

        beta = jnp.sum(jnp.where(lane == h, bg, 0.0), axis=-1, keepdims=True)
        gc = jnp.sum(jnp.where(lane == h + 2 * B_HEADS, bg, 0.0), axis=-1, keepdims=True)
        gcb = jnp.broadcast_to(gc, (t, LANES))
        gct = gcb.T
        row = gct[:t] if t <= LANES else jnp.concatenate([gct] * (t // LANES), axis=0)
        col = jnp.broadcast_to(gc, (t, t))
        dmask = jnp.where(lower, jnp.exp(jnp.where(lower, col - row, 0.0)), 0.0)

        kb = k * beta
        kq = _dot_nt(jnp.concatenate([kb, q], axis=0), k)
        a = jnp.where(strict, kq[:t] * dmask, 0.0)
        qk = kq[t:] * dmask
        eg = jnp.exp(gcb)
        uw = _unit_lower_solve(a, jnp.concatenate([v * beta, kb * eg], axis=1), same_blk, chunk // BD)
        u = uw[:, :B_DV]
        w = uw[:, B_DV:]

        qw = _dot(qk, jnp.concatenate([w, u], axis=1))
        qp = q * eg - qw[:, :B_DV]
        op = qw[:, B_DV:]
        last = [gcb[(c + 1) * chunk - 1:(c + 1) * chunk, :] for c in range(nch)]
        lastb = jnp.concatenate([jnp.broadcast_to(l, (chunk, LANES)) for l in last], axis=0)
        kdec = k * jnp.exp(lastb - gcb)
        wu = jnp.concatenate([jnp.where(chunk_id == c, w, 0.0) for c in range(nch)]
                             + [jnp.where(chunk_id == c, u, 0.0) for c in range(nch)], axis=1)
        xn = _dot_tn(kdec, wu)

        s = s_scr[hh]
        outs = []
        for c in range(nch):
            x_c = xn[:, c * B_DV:(c + 1) * B_DV]
            n_c = xn[:, (nch + c) * B_DV:(nch + c + 1) * B_DV]
            r = _dot(jnp.concatenate([x_c, qp[c * chunk:(c + 1) * chunk]], axis=0), s)
            outs.append(r[B_DK:] + op[c * chunk:(c + 1) * chunk])
            s = s * jnp.exp(last[c]) - r[:B_DK] + n_c
        s_scr[hh] = s
        o = jnp.concatenate(outs, axis=0) if nch > 1 else outs[0]
        on = o * lax.rsqrt(jnp.mean(o * o, axis=-1, keepdims=True) + EPS) * dn
        y_ref[:, cs] = (on * _silu(z_ref[:, cs])).astype(y_ref.dtype)

    @pl.when(n == pl.num_programs(2) - 1)
    def _():
        st_ref[...] = s_scr[...]


def _gated_delta(seg, p, bg, y_in, layer_j, conv_w, tails, dn3, s0):
    t, c = seg.step, seg.chunk
    rb = seg.rb
    wide = HPS * B_DK
    q_blk = (2 * A_QK + 2 * A_V) // wide
    z_blk = (2 * A_QK + 2 * A_V + CONV_CH) // wide
    y_blk = A_V // wide
    hgroups = B_HEADS // HPS

    def pcol(off):
        return pl.BlockSpec((t, wide), lambda b, h, n: (rb(b, n), off + h))

    def wcol(part):
        return pl.BlockSpec((None, CONV_W, wide), lambda b, h, n: (layer_j, 0, part * hgroups + h))

    def tcol(part):
        return pl.BlockSpec((None, 8, wide), lambda b, h, n: (b, 0, part * hgroups + h))

    in_specs = [
        pcol(q_blk), pcol(q_blk + hgroups), pcol(q_blk + 2 * hgroups), pcol(z_blk),
        pl.BlockSpec((t, LANES), lambda b, h, n: (rb(b, n), 0)),
        wcol(0), wcol(1), wcol(2), tcol(0), tcol(1), tcol(2),
        pl.BlockSpec((None, 1, B_DV), lambda b, h, n: (layer_j, 0, 0)),
        pl.BlockSpec((None, HPS, B_DK, B_DV), lambda b, h, n: (b, h, 0, 0)),
        pl.BlockSpec(memory_space=pl.ANY),
    ]
    return pl.pallas_call(
        functools.partial(_delta_kernel, chunk=c, nch=seg.nch),
        out_shape=(jax.ShapeDtypeStruct((M_ALL, D_MODEL), BF16),
                   jax.ShapeDtypeStruct((seg.batch, B_HEADS, B_DK, B_DV), F32)),
        grid=(seg.batch, hgroups, seg.nblk),
        in_specs=in_specs,
        out_specs=(pl.BlockSpec((t, wide), lambda b, h, n: (rb(b, n), y_blk + h)),
                   pl.BlockSpec((None, HPS, B_DK, B_DV), lambda b, h, n: (b, h, 0, 0))),
        scratch_shapes=[pltpu.VMEM((HPS, B_DK, B_DV), F32), pltpu.VMEM((3, 8, wide), F32)],
        input_output_aliases={13: 0},
        compiler_params=_cparams(3),
        name="gated_delta",
    )(p, p, p, p, bg, conv_w, conv_w, conv_w, tails, tails, tails, dn3, s0, y_in)


TQ = 256
BAND = BAND_PREV * CHUNK
WIN = BAND + TQ
NEG = -1e30


def _softmax_pv(s, v):
    m = jnp.max(s, axis=-1, keepdims=True)
    p = jnp.exp(s - m)
    l = jnp.sum(p, axis=-1, keepdims=True)
    return _dot(p, v) / l


def _attn_prompt_kernel(q_ref, k_ref, v_ref, bias_ref, o_ref, ko_ref, vo_ref, kp_scr, vp_scr):
    kp_scr[0:BAND, :] = jnp.zeros((BAND, C_HD), BF16)
    vp_scr[0:BAND, :] = jnp.zeros((BAND, C_HD), BF16)
    kp_scr[BAND:, :] = k_ref[...].astype(BF16)
    vp_scr[BAND:, :] = v_ref[...].astype(BF16)
    ko_ref[...] = k_ref[SEQ - BAND:, :]
    vo_ref[...] = v_ref[SEQ - BAND:, :]

    def block(n, masked):
        start = n * TQ if masked else pl.multiple_of(n * TQ, TQ)
        rows = pl.ds(start, TQ)
        s = _dot_nt(q_ref[rows, :], kp_scr[pl.ds(start, WIN), :]) * (C_HD ** -0.5) + bias_ref[...]
        if masked:
            col = lax.broadcasted_iota(jnp.int32, s.shape, 1)
            s = jnp.where(col + start >= BAND, s, NEG)
        o_ref[rows, :] = _softmax_pv(s, vp_scr[pl.ds(start, WIN), :]).astype(o_ref.dtype)

    n_masked = BAND // TQ
    for n in range(n_masked):
        block(n, True)

    def body(n, carry):
        block(n, False)
        return carry

    lax.fori_loop(n_masked, SEQ // TQ, body, 0)


def _toeplitz_bias(rel_bias, rows, cols, base):
    lv = -(-(rows + cols) // LANES) * LANES
    s = np.arange(lv)
    vec = rel_bias[np.clip(base + rows - 1 - s, -REL_CLIP, REL_CLIP) + REL_CLIP].T
    flat = jnp.tile(vec, (1, rows + 1))[:, :rows * (lv + 1)]
    return flat.reshape(-1, rows, lv + 1)[:, ::-1, :cols]


def _prompt_bias(rel_bias):
    r = np.arange(TQ)
    w = np.arange(WIN)
    j = w[None, :] - (r[:, None] // CHUNK) * CHUNK
    inside = (j >= 0) & (j < BAND + CHUNK)
    return jnp.where(inside[None], _toeplitz_bias(rel_bias, TQ, WIN, BAND), NEG)


def _attention_prompt(p, rel_bias):
    k_blk = C_WIDTH // C_HD
    state = jax.ShapeDtypeStruct((BATCH, BAND, C_WIDTH), F32)
    state_spec = pl.BlockSpec((None, BAND, C_HD), lambda b, h: (b, 0, h))
    return pl.pallas_call(
        _attn_prompt_kernel,
        out_shape=(jax.ShapeDtypeStruct((M_ALL, C_WIDTH), BF16), state, state),
        grid=(BATCH, C_HEADS),
        in_specs=[
            pl.BlockSpec((SEQ, C_HD), lambda b, h: (b, h)),
            pl.BlockSpec((SEQ, C_HD), lambda b, h: (b, k_blk + h)),
            pl.BlockSpec((SEQ, C_HD), lambda b, h: (b, 2 * k_blk + h)),
            pl.BlockSpec((None, TQ, WIN), lambda b, h: (h, 0, 0)),
        ],
        out_specs=(pl.BlockSpec((SEQ, C_HD), lambda b, h: (b, h)), state_spec, state_spec),
        scratch_shapes=[pltpu.VMEM((BAND + SEQ, C_HD), BF16), pltpu.VMEM((BAND + SEQ, C_HD), BF16)],
        compiler_params=_cparams(2),
        name="band_attention_prompt",
    )(p, p, p, _prompt_bias(rel_bias))


HB = 8


def _attn_sample_kernel(q_ref, kn_ref, vn_ref, ck_ref, cv_ref, bias_ref, _, o_ref, ko_ref, vo_ref):
    ko_ref[...] = kn_ref[...]
    vo_ref[...] = vn_ref[...]
    for hh in range(HB):
        cs = slice(hh * C_HD, (hh + 1) * C_HD)
        kk = jnp.concatenate([ck_ref[:, cs], kn_ref[:, cs]], axis=0)
        vv = jnp.concatenate([cv_ref[:, cs], vn_ref[:, cs]], axis=0)
        s = _dot_nt(q_ref[:, cs], kk) * (C_HD ** -0.5) + bias_ref[hh]
        o_ref[:, cs] = _softmax_pv(s, vv).astype(o_ref.dtype)


def _attention_sample(p, o_in, cache_k, cache_v, rel_bias):
    lc = cache_k.shape[1]
    rb0 = M_PROMPT // DEC_SEQ
    wide = HB * C_HD
    k_blk = C_WIDTH // wide
    ck = cache_k.reshape(DEC_BATCH, lc, C_WIDTH)
    cv = cache_v.reshape(DEC_BATCH, lc, C_WIDTH)
    state = jax.ShapeDtypeStruct((DEC_BATCH, DEC_SEQ, C_WIDTH), F32)
    state_spec = pl.BlockSpec((None, DEC_SEQ, wide), lambda b, g: (b, 0, g))
    return pl.pallas_call(
        _attn_sample_kernel,
        out_shape=(jax.ShapeDtypeStruct((M_ALL, C_WIDTH), BF16), state, state),
        grid=(DEC_BATCH, C_HEADS // HB),
        in_specs=[
            pl.BlockSpec((DEC_SEQ, wide), lambda b, g: (rb0 + b, g)),
            pl.BlockSpec((DEC_SEQ, wide), lambda b, g: (rb0 + b, k_blk + g)),
            pl.BlockSpec((DEC_SEQ, wide), lambda b, g: (rb0 + b, 2 * k_blk + g)),
            pl.BlockSpec((None, lc, wide), lambda b, g: (b, 0, g)),
            pl.BlockSpec((None, lc, wide), lambda b, g: (b, 0, g)),
            pl.BlockSpec((HB, DEC_SEQ, lc + DEC_SEQ), lambda b, g: (g, 0, 0)),
            pl.BlockSpec(memory_space=pl.ANY),
        ],
        out_specs=(pl.BlockSpec((DEC_SEQ, wide), lambda b, g: (rb0 + b, g)), state_spec, state_spec),
        input_output_aliases={6: 0},
        compiler_params=_cparams(2),
        name="band_attention_sample",
    )(p, p, p, ck, cv, _toeplitz_bias(rel_bias, DEC_SEQ, lc + DEC_SEQ, lc), o_in)


def _conv_tail(state_conv):
    return jnp.pad(state_conv, ((0, 0), (8 - (CONV_W - 1), 0), (0, 0)))


def _lane_row(x, lanes):
    row = jnp.zeros((LANES,), F32)
    for l0 in lanes:
        row = row.at[l0:l0 + B_HEADS].set(x)
    return row[None, :]


def kernel(x_prompt, x_sample, c_prompt, c_sample, state_ret_0, state_delta_0, state_conv_0, cache_k_1, cache_v_1, state_ret_2, state_delta_2, state_conv_2, cache_k_3, cache_v_3, norm_mix_g, norm_ffn_g, w_mod, b_mod, w_in_even, w_out_even, a_norm_g, b_conv_w, b_a_log, b_dt_bias, b_norm_g, w_in_odd, w_out_odd, c_rel_bias, w_ff1, w_ff2, final_norm_g):
    x = jnp.concatenate([x_prompt.reshape(M_PROMPT, D_MODEL), x_sample.reshape(M_SAMPLE, D_MODEL)], axis=0)
    c_all = jnp.concatenate([c_sample, c_prompt, jnp.zeros((N_SEQ_PAD - N_SEQ, D_MODEL), F32)], axis=0)
    mod4 = _modulation(c_all, w_mod, b_mod).reshape(DEPTH, N_SEQ_PAD, 1, 6 * D_MODEL)

    sample_states = {0: (state_ret_0, state_delta_0, state_conv_0), 1: (cache_k_1, cache_v_1),
                     2: (state_ret_2, state_delta_2, state_conv_2), 3: (cache_k_3, cache_v_3)}
    gn4 = a_norm_g.reshape(-1, A_HEADS, 1, A_DV)
    dn3 = b_norm_g.reshape(-1, 1, B_DV)
    zeros_ret = jnp.zeros((BATCH, A_HEADS, A_DK, A_DV), F32)
    zeros_delta = jnp.zeros((BATCH, B_HEADS, B_DK, B_DV), F32)
    zeros_tail = jnp.zeros((BATCH, 8, CONV_CH), F32)

    p_states, s_states = [], []
    for l in range(DEPTH):
        j = l // 2
        h = _norm_modulate(x, norm_mix_g[l][None, :], mod4, l, 0, 1)
        if l % 2 == 0:
            p = _matmul(h, w_in_even, j, EVEN_MAIN, relu2=False, out_dtype=F32)
            w_ba = w_in_even[j, :, EVEN_MAIN + B_HEADS:]
            w_small = jnp.concatenate([w_in_even[j, :, EVEN_MAIN:], w_ba,
                                       jnp.zeros((D_MODEL, LANES - 3 * B_HEADS), F32)], axis=1).astype(BF16)
            lanes = (B_HEADS, 2 * B_HEADS)
            bg = _beta_gate(h, w_small, _lane_row(b_a_log[j], lanes), _lane_row(b_dt_bias[j], lanes))
            s_ret, s_delta, s_conv = sample_states[l]
            y, p_ret = _retention(SEG_PROMPT, p, None, j, gn4, zeros_ret)
            y, s_ret_new = _retention(SEG_SAMPLE, p, y, j, gn4, s_ret)
            y, p_delta = _gated_delta(SEG_PROMPT, p, bg, y, j, b_conv_w, zeros_tail, dn3, zeros_delta)
            y, s_delta_new = _gated_delta(SEG_SAMPLE, p, bg, y, j, b_conv_w, _conv_tail(s_conv), dn3, s_delta)
            c0 = 2 * A_QK + 2 * A_V
            p_conv = p[:M_PROMPT, c0:c0 + CONV_CH].reshape(BATCH, SEQ, CONV_CH)[:, SEQ - (CONV_W - 1):]
            s_conv_new = p[M_PROMPT:, c0:c0 + CONV_CH].reshape(DEC_BATCH, DEC_SEQ, CONV_CH)[:, DEC_SEQ - (CONV_W - 1):]
            p_states.append((p_ret, p_delta, p_conv))
            s_states.append((s_ret_new, s_delta_new, s_conv_new))
            x = _matmul_residual(y, w_out_even, j, x, mod4, l, 2)
        else:
            p = _matmul(h, w_in_odd, j, 3 * C_WIDTH, relu2=False, out_dtype=F32)
            cache_k, cache_v = sample_states[l]
            o, pk, pv = _attention_prompt(p, c_rel_bias[j])
            o, sk, sv = _attention_sample(p, o, cache_k, cache_v, c_rel_bias[j])
            p_states.append((pk.reshape(BATCH, BAND, C_HEADS, C_HD), pv.reshape(BATCH, BAND, C_HEADS, C_HD)))
            s_states.append((sk.reshape(DEC_BATCH, DEC_SEQ, C_HEADS, C_HD), sv.reshape(DEC_BATCH, DEC_SEQ, C_HEADS, C_HD)))
            x = _matmul_residual(o, w_out_odd, j, x, mod4, l, 2)
        h = _norm_modulate(x, norm_ffn_g[l][None, :], mod4, l, 3, 4)
        f = _matmul(h, w_ff1, l, D_FF, relu2=True, out_dtype=BF16)
        for kblk in range(D_FF // D_MODEL):
            x = _matmul_residual(f, w_ff2, l, x, mod4, l, 5, kblk=kblk)

    g_row = final_norm_g[None, :]
    y_prompt = _final_norm(x, g_row, 0, M_PROMPT).reshape(BATCH, SEQ, D_MODEL)
    y_sample = _final_norm(x, g_row, M_PROMPT, M_SAMPLE).reshape(DEC_BATCH, DEC_SEQ, D_MODEL)
    (p_ret_0, p_delta_0, p_conv_0), (p_k_1, p_v_1), (p_ret_2, p_delta_2, p_conv_2), (p_k_3, p_v_3) = p_states
    (s_ret_0, s_delta_0, s_conv_0), (s_k_1, s_v_1), (s_ret_2, s_delta_2, s_conv_2), (s_k_3, s_v_3) = s_states
    return (y_prompt, y_sample,
            p_ret_0, s_ret_0, p_delta_0, s_delta_0, p_conv_0, s_conv_0,
            p_k_1, s_k_1, p_v_1, s_v_1,
            p_ret_2, s_ret_2, p_delta_2, s_delta_2, p_conv_2, s_conv_2,
            p_k_3, s_k_3, p_v_3, s_v_3)
```

```python
import functools

import numpy as np
import jax
import jax.numpy as jnp
from jax import lax
from jax.experimental import pallas as pl
from jax.experimental.pallas import tpu as pltpu

F32 = jnp.float32
BF16 = jnp.bfloat16

D_MODEL = 4096
BATCH = 4
SEQ = 4096
DEPTH = 4
DEC_BATCH = 32
DEC_SEQ = 32
PAST_LEN = 1024
CHUNK = 64
EPS = 1e-6
A_DK = 128
A_DV = 256
A_HEADS = 8
A_QK = A_HEADS * A_DK
A_V = A_HEADS * A_DV
ROPE_BASE = 10000.0
B_DK = 128
B_DV = 128
B_HEADS = 16
B_QK = B_HEADS * B_DK
B_V = B_HEADS * B_DV
CONV_W = 4
CONV_CH = 2 * B_QK + B_V
EVEN_MAIN = 2 * A_QK + 2 * A_V + CONV_CH + B_V
EVEN_IN = EVEN_MAIN + 2 * B_HEADS
C_HD = 128
C_HEADS = 32
C_WIDTH = C_HEADS * C_HD
BAND_PREV = 8
REL_CLIP = 128
D_FF = 4 * D_MODEL

GROUP = 32
M_PROMPT = BATCH * SEQ
M_SAMPLE = DEC_BATCH * DEC_SEQ
M_ALL = M_PROMPT + M_SAMPLE
N_SEQ = BATCH + DEC_BATCH
N_SEQ_PAD = 40

VMEM_LIMIT = 56 * 1024 * 1024
TM = 1024
TN = 512
LANES = 128
BD = 16


def _cparams(n_axes):
    return pltpu.CompilerParams(dimension_semantics=("arbitrary",) * n_axes, vmem_limit_bytes=VMEM_LIMIT)


def _sigmoid(x):
    return 1.0 / (1.0 + jnp.exp(-x))


def _silu(x):
    return x * _sigmoid(x)


def _dot(a, b):
    return jnp.dot(a.astype(BF16), b.astype(BF16), preferred_element_type=F32)


def _dot_nt(a, b):
    return lax.dot_general(a.astype(BF16), b.astype(BF16), (((1,), (1,)), ((), ())), preferred_element_type=F32)


def _dot_tn(a, b):
    return lax.dot_general(a.astype(BF16), b.astype(BF16), (((0,), (0,)), ((), ())), preferred_element_type=F32)


def _dot_exact_lhs(l_bf16, x):
    d = functools.partial(jnp.dot, preferred_element_type=F32)
    x0 = x.astype(BF16)
    r1 = x - x0.astype(F32)
    x1 = r1.astype(BF16)
    x2 = (r1 - x1.astype(F32)).astype(BF16)
    return d(l_bf16, x0) + (d(l_bf16, x1) + d(l_bf16, x2))


def _mod_kernel(c_ref, w_ref, b_ref, o_ref):
    c = c_ref[...]
    o_ref[...] = _dot(_silu(c), w_ref[...]) + b_ref[...]


def _modulation(c_all, w_mod, b_mod):
    n_out = 6 * D_MODEL
    return pl.pallas_call(
        _mod_kernel,
        out_shape=jax.ShapeDtypeStruct((DEPTH, N_SEQ_PAD, n_out), F32),
        grid=(DEPTH, n_out // TN),
        in_specs=[
            pl.BlockSpec((N_SEQ_PAD, D_MODEL), lambda l, j: (0, 0)),
            pl.BlockSpec((None, D_MODEL, TN), lambda l, j: (l, 0, j)),
            pl.BlockSpec((None, 1, TN), lambda l, j: (l, 0, j)),
        ],
        out_specs=pl.BlockSpec((None, N_SEQ_PAD, TN), lambda l, j: (l, 0, j)),
        compiler_params=_cparams(2),
        name="modulation",
    )(c_all, w_mod, b_mod.reshape(DEPTH, 1, n_out))


def _prompt_mod_spec(width, layer, col, rows_per_tile, tile_of):
    tiles_per_seq = SEQ // rows_per_tile
    return pl.BlockSpec(
        (None, None, 1, width),
        lambda *g: (layer, DEC_BATCH + jnp.minimum(tile_of(*g) // tiles_per_seq, BATCH - 1), 0, col(*g)))


def _sample_mod_spec(width, layer, col, rows_per_tile, tile_of):
    seqs = rows_per_tile // GROUP
    first = M_PROMPT // rows_per_tile
    return pl.BlockSpec(
        (None, seqs, 1, width),
        lambda *g: (layer, jnp.maximum(tile_of(*g) - first, 0), 0, col(*g)))


def _tile_mod(is_sample, gp_ref, gs_ref):
    return jnp.where(is_sample, gs_ref[...], gp_ref[...][None])


NM_ROWS = 256


def _x_specs(x, rows, cols, tile_of, col_of):
    if not isinstance(x, tuple):
        return [pl.BlockSpec((rows, cols), lambda *g: (tile_of(*g), col_of(*g)))]
    last_prompt = M_PROMPT // rows - 1
    return [pl.BlockSpec((rows, cols), lambda *g: (jnp.minimum(tile_of(*g), last_prompt), col_of(*g))),
            pl.BlockSpec((rows, cols), lambda *g: (jnp.maximum(tile_of(*g) - last_prompt - 1, 0), col_of(*g)))]


def _x_tile(is_sample, x_refs):
    if len(x_refs) == 1:
        return x_refs[0][...]
    return jnp.where(is_sample, x_refs[1][...], x_refs[0][...])


def _normmod_kernel(*refs):
    *x_refs, g_ref, shp_ref, shs_ref, scp_ref, scs_ref, o_ref = refs
    is_sample = pl.program_id(0) >= M_PROMPT // NM_ROWS
    x = _x_tile(is_sample, x_refs)
    y = x * lax.rsqrt(jnp.mean(x * x, axis=-1, keepdims=True) + EPS) * g_ref[...]
    y3 = y.reshape(NM_ROWS // GROUP, GROUP, D_MODEL)
    h = y3 * (1.0 + _tile_mod(is_sample, scp_ref, scs_ref)) + _tile_mod(is_sample, shp_ref, shs_ref)
    o_ref[...] = h.reshape(NM_ROWS, D_MODEL).astype(o_ref.dtype)


def _norm_modulate(x, g_row, mod4, layer, shift_col, scale_col):
    def specs(c):
        args = (D_MODEL, layer, lambda i: c, NM_ROWS, lambda i: i)
        return [_prompt_mod_spec(*args), _sample_mod_spec(*args)]

    xs = x if isinstance(x, tuple) else (x,)
    return pl.pallas_call(
        _normmod_kernel,
        out_shape=jax.ShapeDtypeStruct((M_ALL, D_MODEL), BF16),
        grid=(M_ALL // NM_ROWS,),
        in_specs=_x_specs(x, NM_ROWS, D_MODEL, lambda i: i, lambda i: 0)
        + [pl.BlockSpec((1, D_MODEL), lambda i: (0, 0))] + specs(shift_col) + specs(scale_col),
        out_specs=pl.BlockSpec((NM_ROWS, D_MODEL), lambda i: (i, 0)),
        compiler_params=_cparams(1),
        name="norm_modulate",
    )(*xs, g_row, mod4, mod4, mod4, mod4)


def _final_norm_kernel(x_ref, g_ref, o_ref):
    x = x_ref[...]
    o_ref[...] = x * lax.rsqrt(jnp.mean(x * x, axis=-1, keepdims=True) + EPS) * g_ref[...]


def _final_norm(x, g_row, row0, n_rows):
    return pl.pallas_call(
        _final_norm_kernel,
        out_shape=jax.ShapeDtypeStruct((n_rows, D_MODEL), F32),
        grid=(n_rows // NM_ROWS,),
        in_specs=[
            pl.BlockSpec((NM_ROWS, D_MODEL), lambda i: (row0 // NM_ROWS + i, 0)),
            pl.BlockSpec((1, D_MODEL), lambda i: (0, 0)),
        ],
        out_specs=pl.BlockSpec((NM_ROWS, D_MODEL), lambda i: (i, 0)),
        compiler_params=_cparams(1),
        name="final_norm",
    )(x, g_row)


def _cast_weight(w_ref, wb_ref):
    @pl.when(pl.program_id(1) == 0)
    def _():
        wb_ref[...] = w_ref[...].astype(BF16)


def _mm_kernel(a_ref, w_ref, o_ref, wb_ref, *, relu2):
    _cast_weight(w_ref, wb_ref)
    acc = jnp.dot(a_ref[...], wb_ref[...], preferred_element_type=F32)
    if relu2:
        acc = jnp.square(jnp.maximum(acc, 0.0))
    o_ref[...] = acc.astype(o_ref.dtype)


def _matmul(a, w, layer, n_cols, *, relu2, out_dtype):
    k = a.shape[1]
    return pl.pallas_call(
        functools.partial(_mm_kernel, relu2=relu2),
        out_shape=jax.ShapeDtypeStruct((M_ALL, n_cols), out_dtype),
        grid=(n_cols // TN, M_ALL // TM),
        in_specs=[
            pl.BlockSpec((TM, k), lambda j, i: (i, 0)),
            pl.BlockSpec((None, k, TN), lambda j, i: (layer, 0, j)),
        ],
        out_specs=pl.BlockSpec((TM, TN), lambda j, i: (i, j)),
        scratch_shapes=[pltpu.VMEM((k, TN), BF16)],
        compiler_params=_cparams(2),
        name="proj_relu2" if relu2 else "proj",
    )(a, w)


def _mm_res_kernel(*refs):
    a_ref, w_ref, *x_refs, gp_ref, gs_ref, o_ref, wb_ref = refs
    _cast_weight(w_ref, wb_ref)
    acc = jnp.dot(a_ref[...], wb_ref[...], preferred_element_type=F32)
    is_sample = pl.program_id(1) >= M_PROMPT // TM
    gate = _tile_mod(is_sample, gp_ref, gs_ref)
    shape3 = (TM // GROUP, GROUP, TN)
    o_ref[...] = (_x_tile(is_sample, x_refs).reshape(shape3) + gate * acc.reshape(shape3)).reshape(TM, TN)


def _matmul_residual(a, w, w_layer, x, mod4, layer, gate_col, *, kblk=0, kc=D_MODEL):
    nj = D_MODEL // TN
    mod_args = (TN, layer, lambda j, i: gate_col * nj + j, TM, lambda j, i: i)
    xs = x if isinstance(x, tuple) else (x,)
    return pl.pallas_call(
        _mm_res_kernel,
        out_shape=jax.ShapeDtypeStruct((M_ALL, D_MODEL), F32),
        grid=(nj, M_ALL // TM),
        in_specs=[
            pl.BlockSpec((TM, kc), lambda j, i: (i, kblk)),
            pl.BlockSpec((None, kc, TN), lambda j, i: (w_layer, kblk, j)),
        ] + _x_specs(x, TM, TN, lambda j, i: i, lambda j, i: j) + [
            _prompt_mod_spec(*mod_args),
            _sample_mod_spec(*mod_args),
        ],
        out_specs=pl.BlockSpec((TM, TN), lambda j, i: (i, j)),
        scratch_shapes=[pltpu.VMEM((kc, TN), BF16)],
        input_output_aliases={} if isinstance(x, tuple) else {2: 0},
        compiler_params=_cparams(2),
        name="proj_residual",
    )(a, w, *xs, mod4, mod4)


CS_ROWS = 256


def _bg_kernel(a_ref, w_ref, alog_ref, dtb_ref, o_ref):
    acc = jnp.dot(a_ref[...], w_ref[...], preferred_element_type=F32)
    lane = lax.broadcasted_iota(jnp.int32, (CS_ROWS, LANES), 1)
    z = acc + dtb_ref[...]
    softplus = jnp.maximum(z, 0.0) + jnp.log1p(jnp.exp(-jnp.abs(z)))
    g = -jnp.exp(alog_ref[...]) * softplus
    is_sample = pl.program_id(0) * TM >= M_PROMPT
    ii = lax.broadcasted_iota(jnp.int32, (CS_ROWS, CS_ROWS), 0)
    jj = lax.broadcasted_iota(jnp.int32, (CS_ROWS, CS_ROWS), 1)
    chunk_i = jnp.where(is_sample, lax.shift_right_logical(ii, 5), lax.shift_right_logical(ii, 6))
    chunk_j = jnp.where(is_sample, lax.shift_right_logical(jj, 5), lax.shift_right_logical(jj, 6))
    ltri = jnp.where(chunk_i == chunk_j, jnp.where(ii >= jj, 1.0, 0.0), 0.0).astype(BF16)
    for r in range(TM // CS_ROWS):
        rows = slice(r * CS_ROWS, (r + 1) * CS_ROWS)
        gc = _dot_exact_lhs(ltri, g[rows])
        o_ref[rows, :] = jnp.where(lane < B_HEADS, _sigmoid(acc[rows]),
                                   jnp.where(lane < 2 * B_HEADS, g[rows], gc))


def _beta_gate(a, w_small, alog_row, dtb_row):
    return pl.pallas_call(
        _bg_kernel,
        out_shape=jax.ShapeDtypeStruct((M_ALL, LANES), F32),
        grid=(M_ALL // TM,),
        in_specs=[
            pl.BlockSpec((TM, D_MODEL), lambda i: (i, 0)),
            pl.BlockSpec((D_MODEL, LANES), lambda i: (0, 0)),
            pl.BlockSpec((1, LANES), lambda i: (0, 0)),
            pl.BlockSpec((1, LANES), lambda i: (0, 0)),
        ],
        out_specs=pl.BlockSpec((TM, LANES), lambda i: (i, 0)),
        compiler_params=_cparams(1),
        name="beta_gate",
    )(a, w_small, alog_row, dtb_row)


class _Seg:
    def __init__(self, batch, length, row0, step, chunk, pos0, carry=True):
        self.batch, self.length, self.row0, self.step, self.chunk, self.pos0 = batch, length, row0, step, chunk, pos0
        self.nblk = length // step
        self.nch = step // chunk
        self.carry = carry

    def rb(self, b, n):
        return (self.row0 + b * self.length) // self.step + n


SEG_PROMPT = _Seg(BATCH, SEQ, 0, 256, CHUNK, 0)
SEG_SAMPLE = _Seg(DEC_BATCH, DEC_SEQ, M_PROMPT, DEC_SEQ, DEC_SEQ, PAST_LEN)
SAMPLE_SEQS = 8
SEG_SAMPLE_GROUPS = _Seg(DEC_BATCH // SAMPLE_SEQS, SAMPLE_SEQS * DEC_SEQ, M_PROMPT, SAMPLE_SEQS * DEC_SEQ, DEC_SEQ,
                         PAST_LEN, carry=False)


def _chunk_ids(t, chunk, width):
    return lax.shift_right_logical(lax.broadcasted_iota(jnp.int32, (t, width), 0), chunk.bit_length() - 1)


HPA = 2


def _rope(x, cos, sin):
    return x * cos + pltpu.roll(x, A_DK // 2, 1) * sin


def _ret_kernel(*refs, chunk, nch, aliased):
    if aliased:
        refs = refs[:-4] + refs[-3:]
    (q_ref, k_ref, v_ref, g_ref, cos_ref, sin_ref, dm_ref, qd_ref, kd_ref, cd_ref, gn_ref, s0_ref,
     y_ref, st_ref, s_scr) = refs
    n = pl.program_id(2)
    t = chunk * nch

    @pl.when(n == 0)
    def _():
        s_scr[...] = s0_ref[...]

    cos = cos_ref[...]
    sin = sin_ref[...]
    cid_k = _chunk_ids(t, chunk, A_DK)
    cid_v = _chunk_ids(t, chunk, A_DV)
    for hh in range(HPA):
        ks = slice(hh * A_DK, (hh + 1) * A_DK)
        vs = slice(hh * A_DV, (hh + 1) * A_DV)
        q = _rope(q_ref[:, ks], cos, sin)
        k = _rope(k_ref[:, ks], cos, sin) * (A_DK ** -0.5)
        v = v_ref[:, vs]
        scores = _dot_nt(q, k) * dm_ref[hh]
        kdec = k * kd_ref[hh]
        if nch > 1:
            vcat = jnp.concatenate([jnp.where(cid_v == c, v, 0.0) for c in range(nch)], axis=1)
        else:
            vcat = v
        kv = _dot_tn(kdec, vcat)
        cd = cd_ref[hh]
        states = [s_scr[hh]]
        for c in range(nch):
            states.append(states[-1] * cd + kv[:, c * A_DV:(c + 1) * A_DV])
        s_scr[hh] = states[nch]
        qdec = q * qd_ref[hh]
        if nch > 1:
            qcat = jnp.concatenate([jnp.where(cid_k == c, qdec, 0.0) for c in range(nch)], axis=1)
        else:
            qcat = qdec
        o = _dot(jnp.concatenate([qcat, scores], axis=1), jnp.concatenate(states[:nch] + [v], axis=0))
        mu = jnp.mean(o, axis=-1, keepdims=True)
        d = o - mu
        var = jnp.mean(d * d, axis=-1, keepdims=True)
        on = d * lax.rsqrt(var + EPS) * gn_ref[hh]
        y_ref[:, vs] = (_silu(g_ref[:, vs]) * on).astype(y_ref.dtype)

    @pl.when(n == pl.num_programs(2) - 1)
    def _():
        st_ref[...] = s_scr[...]


def _retention_tables(chunk, nch):
    t = chunk * nch
    lg = np.log1p(-np.exp2(-5.0 - np.arange(A_HEADS, dtype=np.float64)))
    idx = np.arange(t)
    pos = idx % chunk
    rel = pos[:, None] - pos[None, :]
    same = (idx[:, None] // chunk) == (idx[None, :] // chunk)
    dm = np.where((same & (rel >= 0))[None], np.exp(np.maximum(rel, 0)[None] * lg[:, None, None]), 0.0)
    qd = np.exp((pos[None, :] + 1.0) * lg[:, None])
    kd = np.exp((chunk - 1.0 - pos)[None, :] * lg[:, None])
    cd = np.exp(chunk * lg)
    bc = lambda x, w: jnp.asarray(np.broadcast_to(x[:, :, None], x.shape + (w,)), F32)
    return jnp.asarray(dm, F32), bc(qd, A_DK), bc(kd, A_DK), bc(cd[:, None], A_DV)


def _rope_tables(pos0, length):
    half = A_DK // 2
    inv_freq = ROPE_BASE ** (-np.arange(half, dtype=np.float64) / half)
    ang = (pos0 + np.arange(length, dtype=np.float64))[:, None] * inv_freq[None, :]
    cos, sin = np.cos(ang), np.sin(ang)
    return (jnp.asarray(np.concatenate([cos, cos], axis=-1), F32),
            jnp.asarray(np.concatenate([-sin, sin], axis=-1), F32))


def _retention(seg, p, y_in, layer_j, gn4, s0):
    t, c = seg.step, seg.chunk
    dm, qd, kd, cd = _retention_tables(c, seg.nch)
    cos, sin = _rope_tables(seg.pos0, seg.length)
    rb = seg.rb
    kw, vw = HPA * A_DK, HPA * A_DV
    hgroups = A_HEADS // HPA
    k_blk = A_QK // kw
    v_blk = 2 * A_QK // vw
    g_blk = (2 * A_QK + A_V) // vw
    in_specs = [
        pl.BlockSpec((t, kw), lambda b, h, n: (rb(b, n), h)),
        pl.BlockSpec((t, kw), lambda b, h, n: (rb(b, n), k_blk + h)),
        pl.BlockSpec((t, vw), lambda b, h, n: (rb(b, n), v_blk + h)),
        pl.BlockSpec((t, vw), lambda b, h, n: (rb(b, n), g_blk + h)),
        pl.BlockSpec((t, A_DK), lambda b, h, n: (n, 0)),
        pl.BlockSpec((t, A_DK), lambda b, h, n: (n, 0)),
        pl.BlockSpec((HPA, t, t), lambda b, h, n: (h, 0, 0)),
        pl.BlockSpec((HPA, t, A_DK), lambda b, h, n: (h, 0, 0)),
        pl.BlockSpec((HPA, t, A_DK), lambda b, h, n: (h, 0, 0)),
        pl.BlockSpec((HPA, 1, A_DV), lambda b, h, n: (h, 0, 0)),
        pl.BlockSpec((None, HPA, 1, A_DV), lambda b, h, n: (layer_j, h, 0, 0)),
        pl.BlockSpec((None, HPA, A_DK, A_DV), lambda b, h, n: (b, h, 0, 0)),
    ]
    args = [p, p, p, p, cos, sin, dm, qd, kd, cd, gn4, s0]
    aliases = {}
    if y_in is not None:
        in_specs.append(pl.BlockSpec(memory_space=pl.ANY))
        args.append(y_in)
        aliases = {len(args) - 1: 0}
    return pl.pallas_call(
        functools.partial(_ret_kernel, chunk=c, nch=seg.nch, aliased=y_in is not None),
        out_shape=(jax.ShapeDtypeStruct((M_ALL, D_MODEL), BF16),
                   jax.ShapeDtypeStruct((seg.batch, A_HEADS, A_DK, A_DV), F32)),
        grid=(seg.batch, hgroups, seg.nblk),
        in_specs=in_specs,
        out_specs=(pl.BlockSpec((t, vw), lambda b, h, n: (rb(b, n), h)),
                   pl.BlockSpec((None, HPA, A_DK, A_DV), lambda b, h, n: (b, h, 0, 0))),
        scratch_shapes=[pltpu.VMEM((HPA, A_DK, A_DV), F32)],
        input_output_aliases=aliases,
        compiler_params=_cparams(3),
        name="retention",
    )(*args)


HPS = 8


def _unit_lower_solve(a, rhs, same_blk, nblocks):
    hs = range(len(a))
    n = a[0].shape[0]
    d = [jnp.where(same_blk, a[h], 0.0) for h in hs]
    y = [jnp.concatenate([a[h] - d[h], rhs[h]], axis=1) for h in hs]
    sign = -1.0
    for _ in range(BD.bit_length() - 2):
        r = [_dot(d[h], jnp.concatenate([d[h], y[h]], axis=1)) for h in hs]
        d = [r[h][:, :n] for h in hs]
        y = [y[h] + sign * r[h][:, n:] for h in hs]
        sign = 1.0
    y = [y[h] + _dot(d[h], y[h]) for h in hs]
    f = [y[h][:, :n] for h in hs]
    y = [y[h][:, n:] for h in hs]
    if nblocks == 1:
        return y
    if nblocks == 2:
        return [y[h] - _dot(f[h], y[h]) for h in hs]
    r = [_dot(f[h], jnp.concatenate([f[h], y[h]], axis=1)) for h in hs]
    f2 = [r[h][:, :n] for h in hs]
    y = [y[h] - r[h][:, n:] for h in hs]
    return [y[h] + _dot(f2[h], y[h]) for h in hs]


def _delta_kernel(*refs, chunk, nch, carry):
    (q_ref, k_ref, v_ref, z_ref, bg_ref, cwq_ref, cwk_ref, cwv_ref, tq_ref, tk_ref, tv_ref, dn_ref, s0_ref,
     _, y_ref, st_ref, s_scr, tail_scr) = refs
    hg = pl.program_id(1)
    n = pl.program_id(2)
    t = chunk * nch
    log2_chunk = chunk.bit_length() - 1
    tail_refs = (tq_ref, tk_ref, tv_ref)
    hs = range(HPS)
    cols = [slice(hh * B_DK, (hh + 1) * B_DK) for hh in hs]

    if carry:
        @pl.when(n == 0)
        def _():
            s_scr[...] = s0_ref[...]
            for part in range(3):
                tail_scr[part] = tail_refs[part][...]

    ii = lax.broadcasted_iota(jnp.int32, (t, t), 0)
    jj = lax.broadcasted_iota(jnp.int32, (t, t), 1)
    same_chunk = lax.shift_right_logical(ii, log2_chunk) == lax.shift_right_logical(jj, log2_chunk)
    lower = jnp.logical_and(same_chunk, ii >= jj)
    strict = jnp.logical_and(same_chunk, ii > jj)
    same_blk = lax.shift_right_logical(ii, BD.bit_length() - 1) == lax.shift_right_logical(jj, BD.bit_length() - 1)
    chunk_id = _chunk_ids(t, chunk, LANES)
    bg = bg_ref[...]
    lane = lax.broadcasted_iota(jnp.int32, bg.shape, 1)
    dn = dn_ref[...]

    def conv(u_ref, cw_ref, part, cs):
        u = u_ref[:, cs]
        w = cw_ref[:, cs]
        out = u * w[CONV_W - 1:CONV_W, :]
        first = 8 - (CONV_W - 1)
        if carry:
            ext = jnp.concatenate([tail_scr[part, :, cs], u], axis=0)
            for i in range(CONV_W - 1):
                out = out + ext[first + i:first + i + t, :] * w[i:i + 1, :]
            tail_scr[part, :, cs] = u[t - 8:t, :]
        else:
            ext = jnp.concatenate([tail_refs[part][:, :, cs], u.reshape(nch, chunk, B_DK)], axis=1)
            for i in range(CONV_W - 1):
                out = out + (ext[:, first + i:first + i + chunk, :] * w[i:i + 1, :][None]).reshape(t, B_DK)
        return _silu(out)

    cq = [conv(q_ref, cwq_ref, 0, cs) for cs in cols]
    ck = [conv(k_ref, cwk_ref, 1, cs) for cs in cols]
    v = [conv(v_ref, cwv_ref, 2, cs) for cs in cols]
    q = [x * lax.rsqrt(jnp.sum(x * x, axis=-1, keepdims=True) + EPS) * (B_DK ** -0.5) for x in cq]
    k = [x * lax.rsqrt(jnp.sum(x * x, axis=-1, keepdims=True) + EPS) for x in ck]

    def head_column(lane0, hh):
        return jnp.sum(jnp.where(lane == lane0 + hg * HPS + hh, bg, 0.0), axis=-1, keepdims=True)

    beta = [head_column(0, hh) for hh in hs]
    gc = [head_column(2 * B_HEADS, hh) for hh in hs]
    gcb = [jnp.broadcast_to(x, (t, LANES)) for x in gc]

    def decay_mask(hh):
        gct = gcb[hh].T
        row = gct[:t] if t <= LANES else jnp.concatenate([gct] * (t // LANES), axis=0)
        col = jnp.broadcast_to(gc[hh], (t, t))
        return jnp.where(lower, jnp.exp(col - row), 0.0)

    dmask = [decay_mask(hh) for hh in hs]
    kb = [k[hh] * beta[hh] for hh in hs]
    kq = [_dot_nt(jnp.concatenate([kb[hh], q[hh]], axis=0), k[hh]) for hh in hs]
    a = [jnp.where(strict, kq[hh][:t] * dmask[hh], 0.0) for hh in hs]
    qk = [kq[hh][t:] * dmask[hh] for hh in hs]
    eg = [jnp.exp(x) for x in gcb]
    uw = _unit_lower_solve(a, [jnp.concatenate([v[hh] * beta[hh], kb[hh] * eg[hh]], axis=1) for hh in hs],
                           same_blk, chunk // BD)
    u = [x[:, :B_DV] for x in uw]
    w = [x[:, B_DV:] for x in uw]

    qw = [_dot(qk[hh], jnp.concatenate([w[hh], u[hh]], axis=1)) for hh in hs]
    qp = [q[hh] * eg[hh] - qw[hh][:, :B_DV] for hh in hs]
    op = [x[:, B_DV:] for x in qw]
    last = [[gcb[hh][(c + 1) * chunk - 1:(c + 1) * chunk, :] for c in range(nch)] for hh in hs]
    lastb = [jnp.concatenate([jnp.broadcast_to(l, (chunk, LANES)) for l in last[hh]], axis=0) for hh in hs]
    kdec = [k[hh] * jnp.exp(lastb[hh] - gcb[hh]) for hh in hs]
    wu = [jnp.concatenate([jnp.where(chunk_id == c, w[hh], 0.0) for c in range(nch)]
                          + [jnp.where(chunk_id == c, u[hh], 0.0) for c in range(nch)], axis=1) for hh in hs]
    xn = [_dot_tn(kdec[hh], wu[hh]) for hh in hs]

    s = [s_scr[hh] if carry else None for hh in hs]
    outs = [[] for _ in hs]
    for c in range(nch):
        rows = slice(c * chunk, (c + 1) * chunk)
        if not carry:
            s = [s0_ref[c, hh] for hh in hs]
        r = [_dot(jnp.concatenate([xn[hh][:, c * B_DV:(c + 1) * B_DV], qp[hh][rows]], axis=0), s[hh]) for hh in hs]
        for hh in hs:
            outs[hh].append(r[hh][B_DK:] + op[hh][rows])
        s = [s[hh] * jnp.exp(last[hh][c]) - r[hh][:B_DK] + xn[hh][:, (nch + c) * B_DV:(nch + c + 1) * B_DV]
             for hh in hs]
        if not carry:
            for hh in hs:
                st_ref[c, hh] = s[hh]
    for hh in hs:
        if carry:
            s_scr[hh] = s[hh]
        o = jnp.concatenate(outs[hh], axis=0) if nch > 1 else outs[hh][0]
        on = o * lax.rsqrt(jnp.mean(o * o, axis=-1, keepdims=True) + EPS) * dn
        y_ref[:, cols[hh]] = (on * _silu(z_ref[:, cols[hh]])).astype(y_ref.dtype)

    if carry:
        @pl.when(n == pl.num_programs(2) - 1)
        def _():
            st_ref[...] = s_scr[...]


def _gated_delta(seg, p, bg, y_in, layer_j, conv_w, tails, dn3, s0):
    t, c = seg.step, seg.chunk
    rb = seg.rb
    wide = HPS * B_DK
    q_blk = (2 * A_QK + 2 * A_V) // wide
    z_blk = (2 * A_QK + 2 * A_V + CONV_CH) // wide
    y_blk = A_V // wide
    hgroups = B_HEADS // HPS
    seqs = None if seg.carry else seg.nch

    def pcol(off):
        return pl.BlockSpec((t, wide), lambda b, h, n: (rb(b, n), off + h))

    def wcol(part):
        return pl.BlockSpec((None, CONV_W, wide), lambda b, h, n: (layer_j, 0, part * hgroups + h))

    def tcol(part):
        return pl.BlockSpec((seqs, 8, wide), lambda b, h, n: (b, 0, part * hgroups + h))

    state_spec = pl.BlockSpec((seqs, HPS, B_DK, B_DV), lambda b, h, n: (b, h, 0, 0))
    in_specs = [
        pcol(q_blk), pcol(q_blk + hgroups), pcol(q_blk + 2 * hgroups), pcol(z_blk),
        pl.BlockSpec((t, LANES), lambda b, h, n: (rb(b, n), 0)),
        wcol(0), wcol(1), wcol(2), tcol(0), tcol(1), tcol(2),
        pl.BlockSpec((None, 1, B_DV), lambda b, h, n: (layer_j, 0, 0)),
        state_spec,
        pl.BlockSpec(memory_space=pl.ANY),
    ]
    return pl.pallas_call(
        functools.partial(_delta_kernel, chunk=c, nch=seg.nch, carry=seg.carry),
        out_shape=(jax.ShapeDtypeStruct((M_ALL, D_MODEL), BF16), jax.ShapeDtypeStruct(s0.shape, F32)),
        grid=(seg.batch, hgroups, seg.nblk),
        in_specs=in_specs,
        out_specs=(pl.BlockSpec((t, wide), lambda b, h, n: (rb(b, n), y_blk + h)), state_spec),
        scratch_shapes=[pltpu.VMEM((HPS, B_DK, B_DV), F32), pltpu.VMEM((3, 8, wide), F32)],
        input_output_aliases={13: 0},
        compiler_params=_cparams(3),
        name="gated_delta",
    )(p, p, p, p, bg, conv_w, conv_w, conv_w, tails, tails, tails, dn3, s0, y_in)


TQ = 256
BAND = BAND_PREV * CHUNK
WIN = BAND + TQ
NEG = -1e30
ATTN_UNROLL = 7


def _softmax_pv(s, v):
    m = jnp.max(s, axis=-1, keepdims=True)
    p = jnp.exp(s - m)
    l = jnp.sum(p, axis=-1, keepdims=True)
    return _dot(p, v) / l


BIAS_LANES = 1024


def _bias_vectors(rel_bias, cols, base):
    s = np.arange(BIAS_LANES)
    d = np.where(s < cols, s, s - BIAS_LANES)
    return rel_bias[np.clip(base - d, -REL_CLIP, REL_CLIP) + REL_CLIP].T[:, None, :]


def _toeplitz(vec, rows, cols):
    x = jnp.broadcast_to(vec, (rows, BIAS_LANES))
    return pltpu.roll(x, 0, 1, stride=1, stride_axis=0)[:, :cols]


def _attn_prompt_kernel(q_ref, k_ref, v_ref, vec_ref, o_ref, kp_scr, vp_scr, bias_scr):
    kp_scr[0:BAND, :] = jnp.zeros((BAND, C_HD), BF16)
    vp_scr[0:BAND, :] = jnp.zeros((BAND, C_HD), BF16)
    kp_scr[BAND:, :] = k_ref[...].astype(BF16)
    vp_scr[BAND:, :] = v_ref[...].astype(BF16)
    r = lax.broadcasted_iota(jnp.int32, (TQ, WIN), 0)
    w = lax.broadcasted_iota(jnp.int32, (TQ, WIN), 1)
    j = w - lax.shift_left(lax.shift_right_logical(r, CHUNK.bit_length() - 1), CHUNK.bit_length() - 1)
    inside = jnp.logical_and(j >= 0, j < BAND + CHUNK)
    bias_scr[...] = jnp.where(inside, _toeplitz(vec_ref[...], TQ, WIN), NEG)

    def block(n, masked):
        start = n * TQ if masked else pl.multiple_of(n * TQ, TQ)
        rows = pl.ds(start, TQ)
        s = _dot_nt(q_ref[rows, :], kp_scr[pl.ds(start, WIN), :]) * (C_HD ** -0.5) + bias_scr[...]
        if masked:
            col = lax.broadcasted_iota(jnp.int32, s.shape, 1)
            s = jnp.where(col + start >= BAND, s, NEG)
        o_ref[rows, :] = _softmax_pv(s, vp_scr[pl.ds(start, WIN), :]).astype(o_ref.dtype)

    n_masked = BAND // TQ
    for n in range(n_masked):
        block(n, True)

    def body(n, carry):
        block(n, False)
        return carry

    lax.fori_loop(n_masked, SEQ // TQ, body, 0, unroll=ATTN_UNROLL)


def _attention_prompt(p, rel_bias):
    k_blk = C_WIDTH // C_HD
    return pl.pallas_call(
        _attn_prompt_kernel,
        out_shape=jax.ShapeDtypeStruct((M_ALL, C_WIDTH), BF16),
        grid=(BATCH, C_HEADS),
        in_specs=[
            pl.BlockSpec((SEQ, C_HD), lambda b, h: (b, h)),
            pl.BlockSpec((SEQ, C_HD), lambda b, h: (b, k_blk + h)),
            pl.BlockSpec((SEQ, C_HD), lambda b, h: (b, 2 * k_blk + h)),
            pl.BlockSpec((None, 1, BIAS_LANES), lambda b, h: (h, 0, 0)),
        ],
        out_specs=pl.BlockSpec((SEQ, C_HD), lambda b, h: (b, h)),
        scratch_shapes=[pltpu.VMEM((BAND + SEQ, C_HD), BF16), pltpu.VMEM((BAND + SEQ, C_HD), BF16),
                        pltpu.VMEM((TQ, WIN), F32)],
        compiler_params=_cparams(2),
        name="band_attention_prompt",
    )(p, p, p, _bias_vectors(rel_bias, WIN, BAND))


HB = 8


def _band_state_kernel(k_ref, v_ref, ko_ref, vo_ref):
    for hh in range(HB):
        cs = slice(hh * C_HD, (hh + 1) * C_HD)
        ko_ref[:, hh, :] = k_ref[:, cs]
        vo_ref[:, hh, :] = v_ref[:, cs]


def _band_state(p):
    wide = HB * C_HD
    k_blk = C_WIDTH // wide
    last = SEQ // BAND - 1
    state = jax.ShapeDtypeStruct((BATCH, BAND, C_HEADS, C_HD), F32)
    state_spec = pl.BlockSpec((None, BAND, HB, C_HD), lambda b, g: (b, 0, g, 0))
    return pl.pallas_call(
        _band_state_kernel,
        out_shape=(state, state),
        grid=(BATCH, C_HEADS // HB),
        in_specs=[
            pl.BlockSpec((BAND, wide), lambda b, g: (b * (SEQ // BAND) + last, k_blk + g)),
            pl.BlockSpec((BAND, wide), lambda b, g: (b * (SEQ // BAND) + last, 2 * k_blk + g)),
        ],
        out_specs=(state_spec, state_spec),
        compiler_params=_cparams(2),
        name="band_state",
    )(p, p)


def _attn_sample_kernel(q_ref, kn_ref, vn_ref, ck_ref, cv_ref, vec_ref, _, o_ref, ko_ref, vo_ref):
    lc = ck_ref.shape[0]
    for hh in range(HB):
        cs = slice(hh * C_HD, (hh + 1) * C_HD)
        kn = kn_ref[:, cs]
        vn = vn_ref[:, cs]
        ko_ref[:, hh, :] = kn
        vo_ref[:, hh, :] = vn
        kk = jnp.concatenate([ck_ref[:, hh, :], kn], axis=0)
        vv = jnp.concatenate([cv_ref[:, hh, :], vn], axis=0)
        bias = _toeplitz(vec_ref[hh], DEC_SEQ, lc + DEC_SEQ)
        s = _dot_nt(q_ref[:, cs], kk) * (C_HD ** -0.5) + bias
        o_ref[:, cs] = _softmax_pv(s, vv).astype(o_ref.dtype)


def _attention_sample(p, o_in, cache_k, cache_v, rel_bias):
    lc = cache_k.shape[1]
    rb0 = M_PROMPT // DEC_SEQ
    wide = HB * C_HD
    k_blk = C_WIDTH // wide
    state = jax.ShapeDtypeStruct((DEC_BATCH, DEC_SEQ, C_HEADS, C_HD), F32)
    state_spec = pl.BlockSpec((None, DEC_SEQ, HB, C_HD), lambda b, g: (b, 0, g, 0))
    cache_spec = pl.BlockSpec((None, lc, HB, C_HD), lambda b, g: (b, 0, g, 0))
    return pl.pallas_call(
        _attn_sample_kernel,
        out_shape=(jax.ShapeDtypeStruct((M_ALL, C_WIDTH), BF16), state, state),
        grid=(DEC_BATCH, C_HEADS // HB),
        in_specs=[
            pl.BlockSpec((DEC_SEQ, wide), lambda b, g: (rb0 + b, g)),
            pl.BlockSpec((DEC_SEQ, wide), lambda b, g: (rb0 + b, k_blk + g)),
            pl.BlockSpec((DEC_SEQ, wide), lambda b, g: (rb0 + b, 2 * k_blk + g)),
            cache_spec, cache_spec,
            pl.BlockSpec((HB, 1, BIAS_LANES), lambda b, g: (g, 0, 0)),
            pl.BlockSpec(memory_space=pl.ANY),
        ],
        out_specs=(pl.BlockSpec((DEC_SEQ, wide), lambda b, g: (rb0 + b, g)), state_spec, state_spec),
        input_output_aliases={6: 0},
        compiler_params=_cparams(2),
        name="band_attention_sample",
    )(p, p, p, cache_k, cache_v, _bias_vectors(rel_bias, lc + DEC_SEQ, lc), o_in)


def _conv_tail(state_conv):
    return jnp.pad(state_conv, ((0, 0), (8 - (CONV_W - 1), 0), (0, 0)))


def _lane_row(x, lanes):
    row = jnp.zeros((LANES,), F32)
    for l0 in lanes:
        row = row.at[l0:l0 + B_HEADS].set(x)
    return row[None, :]


def kernel(x_prompt, x_sample, c_prompt, c_sample, state_ret_0, state_delta_0, state_conv_0, cache_k_1, cache_v_1, state_ret_2, state_delta_2, state_conv_2, cache_k_3, cache_v_3, norm_mix_g, norm_ffn_g, w_mod, b_mod, w_in_even, w_out_even, a_norm_g, b_conv_w, b_a_log, b_dt_bias, b_norm_g, w_in_odd, w_out_odd, c_rel_bias, w_ff1, w_ff2, final_norm_g):
    x = (x_prompt.reshape(M_PROMPT, D_MODEL), x_sample.reshape(M_SAMPLE, D_MODEL))
    c_all = jnp.concatenate([c_sample, c_prompt, jnp.zeros((N_SEQ_PAD - N_SEQ, D_MODEL), F32)], axis=0)
    mod4 = _modulation(c_all, w_mod, b_mod).reshape(DEPTH, N_SEQ_PAD, 1, 6 * D_MODEL)

    sample_states = {0: (state_ret_0, state_delta_0, state_conv_0), 1: (cache_k_1, cache_v_1),
                     2: (state_ret_2, state_delta_2, state_conv_2), 3: (cache_k_3, cache_v_3)}
    gn4 = a_norm_g.reshape(-1, A_HEADS, 1, A_DV)
    dn3 = b_norm_g.reshape(-1, 1, B_DV)
    zeros_ret = jnp.zeros((BATCH, A_HEADS, A_DK, A_DV), F32)
    zeros_delta = jnp.zeros((BATCH, B_HEADS, B_DK, B_DV), F32)
    zeros_tail = jnp.zeros((BATCH, 8, CONV_CH), F32)

    p_states, s_states = [], []
    for l in range(DEPTH):
        j = l // 2
        h = _norm_modulate(x, norm_mix_g[l][None, :], mod4, l, 0, 1)
        if l % 2 == 0:
            p = _matmul(h, w_in_even, j, EVEN_MAIN, relu2=False, out_dtype=F32)
            w_ba = w_in_even[j, :, EVEN_MAIN + B_HEADS:]
            w_small = jnp.concatenate([w_in_even[j, :, EVEN_MAIN:], w_ba,
                                       jnp.zeros((D_MODEL, LANES - 3 * B_HEADS), F32)], axis=1).astype(BF16)
            lanes = (B_HEADS, 2 * B_HEADS)
            bg = _beta_gate(h, w_small, _lane_row(b_a_log[j], lanes), _lane_row(b_dt_bias[j], lanes))
            s_ret, s_delta, s_conv = sample_states[l]
            y, p_ret = _retention(SEG_PROMPT, p, None, j, gn4, zeros_ret)
            y, s_ret_new = _retention(SEG_SAMPLE, p, y, j, gn4, s_ret)
            y, p_delta = _gated_delta(SEG_PROMPT, p, bg, y, j, b_conv_w, zeros_tail, dn3, zeros_delta)
            y, s_delta_new = _gated_delta(SEG_SAMPLE_GROUPS, p, bg, y, j, b_conv_w, _conv_tail(s_conv), dn3, s_delta)
            c0 = 2 * A_QK + 2 * A_V
            p_conv = jnp.stack([lax.slice(p, ((b + 1) * SEQ - (CONV_W - 1), c0), ((b + 1) * SEQ, c0 + CONV_CH))
                                for b in range(BATCH)])
            s_conv_new = lax.slice(p, (M_PROMPT, c0), (M_ALL, c0 + CONV_CH)).reshape(
                DEC_BATCH, DEC_SEQ, CONV_CH)[:, DEC_SEQ - (CONV_W - 1):]
            p_states.append((p_ret, p_delta, p_conv))
            s_states.append((s_ret_new, s_delta_new, s_conv_new))
            x = _matmul_residual(y, w_out_even, j, x, mod4, l, 2)
        else:
            p = _matmul(h, w_in_odd, j, 3 * C_WIDTH, relu2=False, out_dtype=F32)
            cache_k, cache_v = sample_states[l]
            o = _attention_prompt(p, c_rel_bias[j])
            o, sk, sv = _attention_sample(p, o, cache_k, cache_v, c_rel_bias[j])
            p_states.append(_band_state(p))
            s_states.append((sk, sv))
            x = _matmul_residual(o, w_out_odd, j, x, mod4, l, 2)
        h = _norm_modulate(x, norm_ffn_g[l][None, :], mod4, l, 3, 4)
        f = _matmul(h, w_ff1, l, D_FF, relu2=True, out_dtype=BF16)
        for kblk in range(D_FF // D_MODEL):
            x = _matmul_residual(f, w_ff2, l, x, mod4, l, 5, kblk=kblk)

    g_row = final_norm_g[None, :]
    y_prompt = _final_norm(x, g_row, 0, M_PROMPT).reshape(BATCH, SEQ, D_MODEL)
    y_sample = _final_norm(x, g_row, M_PROMPT, M_SAMPLE).reshape(DEC_BATCH, DEC_SEQ, D_MODEL)
    (p_ret_0, p_delta_0, p_conv_0), (p_k_1, p_v_1), (p_ret_2, p_delta_2, p_conv_2), (p_k_3, p_v_3) = p_states
    (s_ret_0, s_delta_0, s_conv_0), (s_k_1, s_v_1), (s_ret_2, s_delta_2, s_conv_2), (s_k_3, s_v_3) = s_states
    return (y_prompt, y_sample,
            p_ret_0, s_ret_0, p_delta_0, s_delta_0, p_conv_0, s_conv_0,
            p_k_1, s_k_1, p_v_1, s_v_1,
            p_ret_2, s_ret_2, p_delta_2, s_delta_2, p_conv_2, s_conv_2,
            p_k_3, s_k_3, p_v_3, s_v_3)
```

```python
import functools

import numpy as np
import jax
import jax.numpy as jnp
from jax import lax
from jax.experimental import pallas as pl
from jax.experimental.pallas import tpu as pltpu

F32 = jnp.float32
BF16 = jnp.bfloat16

D_MODEL = 4096
BATCH = 4
SEQ = 4096
DEPTH = 4
DEC_BATCH = 32
DEC_SEQ = 32
PAST_LEN = 1024
CHUNK = 64
EPS = 1e-6
A_DK = 128
A_DV = 256
A_HEADS = 8
A_QK = A_HEADS * A_DK
A_V = A_HEADS * A_DV
ROPE_BASE = 10000.0
B_DK = 128
B_DV = 128
B_HEADS = 16
B_QK = B_HEADS * B_DK
B_V = B_HEADS * B_DV
CONV_W = 4
CONV_CH = 2 * B_QK + B_V
EVEN_MAIN = 2 * A_QK + 2 * A_V + CONV_CH + B_V
EVEN_IN = EVEN_MAIN + 2 * B_HEADS
C_HD = 128
C_HEADS = 32
C_WIDTH = C_HEADS * C_HD
BAND_PREV = 8
REL_CLIP = 128
D_FF = 4 * D_MODEL

GROUP = 32
M_PROMPT = BATCH * SEQ
M_SAMPLE = DEC_BATCH * DEC_SEQ
M_ALL = M_PROMPT + M_SAMPLE
N_SEQ = BATCH + DEC_BATCH
N_SEQ_PAD = 40

VMEM_LIMIT = 56 * 1024 * 1024
TM = 1024
TN = 512
LANES = 128
BD = 16


def _cparams(n_axes):
    return pltpu.CompilerParams(dimension_semantics=("arbitrary",) * n_axes, vmem_limit_bytes=VMEM_LIMIT)


def _sigmoid(x):
    return 1.0 / (1.0 + jnp.exp(-x))


def _silu(x):
    return x * _sigmoid(x)


def _dot(a, b):
    return jnp.dot(a.astype(BF16), b.astype(BF16), preferred_element_type=F32)


def _dot_nt(a, b):
    return lax.dot_general(a.astype(BF16), b.astype(BF16), (((1,), (1,)), ((), ())), preferred_element_type=F32)


def _dot_tn(a, b):
    return lax.dot_general(a.astype(BF16), b.astype(BF16), (((0,), (0,)), ((), ())), preferred_element_type=F32)


def _dot_exact_lhs(l_bf16, x):
    d = functools.partial(jnp.dot, preferred_element_type=F32)
    x0 = x.astype(BF16)
    r1 = x - x0.astype(F32)
    x1 = r1.astype(BF16)
    x2 = (r1 - x1.astype(F32)).astype(BF16)
    return d(l_bf16, x0) + (d(l_bf16, x1) + d(l_bf16, x2))


def _mod_kernel(c_ref, w_ref, b_ref, o_ref):
    c = c_ref[...]
    o_ref[...] = _dot(_silu(c), w_ref[...]) + b_ref[...]


def _modulation(c_all, w_mod, b_mod):
    n_out = 6 * D_MODEL
    return pl.pallas_call(
        _mod_kernel,
        out_shape=jax.ShapeDtypeStruct((DEPTH, N_SEQ_PAD, n_out), F32),
        grid=(DEPTH, n_out // TN),
        in_specs=[
            pl.BlockSpec((N_SEQ_PAD, D_MODEL), lambda l, j: (0, 0)),
            pl.BlockSpec((None, D_MODEL, TN), lambda l, j: (l, 0, j)),
            pl.BlockSpec((None, 1, TN), lambda l, j: (l, 0, j)),
        ],
        out_specs=pl.BlockSpec((None, N_SEQ_PAD, TN), lambda l, j: (l, 0, j)),
        compiler_params=_cparams(2),
        name="modulation",
    )(c_all, w_mod, b_mod.reshape(DEPTH, 1, n_out))


def _prompt_mod_spec(width, layer, col, rows_per_tile, tile_of):
    tiles_per_seq = SEQ // rows_per_tile
    return pl.BlockSpec(
        (None, None, 1, width),
        lambda *g: (layer, DEC_BATCH + jnp.minimum(tile_of(*g) // tiles_per_seq, BATCH - 1), 0, col(*g)))


def _sample_mod_spec(width, layer, col, rows_per_tile, tile_of):
    seqs = rows_per_tile // GROUP
    first = M_PROMPT // rows_per_tile
    return pl.BlockSpec(
        (None, seqs, 1, width),
        lambda *g: (layer, jnp.maximum(tile_of(*g) - first, 0), 0, col(*g)))


def _tile_mod(is_sample, gp_ref, gs_ref):
    return jnp.where(is_sample, gs_ref[...], gp_ref[...][None])


NM_ROWS = 512


def _x_specs(x, rows, cols, tile_of, col_of):
    if not isinstance(x, tuple):
        return [pl.BlockSpec((rows, cols), lambda *g: (tile_of(*g), col_of(*g)))]
    last_prompt = M_PROMPT // rows - 1
    return [pl.BlockSpec((rows, cols), lambda *g: (jnp.minimum(tile_of(*g), last_prompt), col_of(*g))),
            pl.BlockSpec((rows, cols), lambda *g: (jnp.maximum(tile_of(*g) - last_prompt - 1, 0), col_of(*g)))]


def _x_tile(is_sample, x_refs):
    if len(x_refs) == 1:
        return x_refs[0][...]
    return jnp.where(is_sample, x_refs[1][...], x_refs[0][...])


def _normmod_kernel(*refs):
    *x_refs, g_ref, shp_ref, shs_ref, scp_ref, scs_ref, o_ref = refs
    is_sample = pl.program_id(0) >= M_PROMPT // NM_ROWS
    x = _x_tile(is_sample, x_refs)
    y = x * lax.rsqrt(jnp.mean(x * x, axis=-1, keepdims=True) + EPS) * g_ref[...]
    y3 = y.reshape(NM_ROWS // GROUP, GROUP, D_MODEL)
    h = y3 * (1.0 + _tile_mod(is_sample, scp_ref, scs_ref)) + _tile_mod(is_sample, shp_ref, shs_ref)
    o_ref[...] = h.reshape(NM_ROWS, D_MODEL).astype(o_ref.dtype)


def _norm_modulate(x, g_row, mod4, layer, shift_col, scale_col):
    def specs(c):
        args = (D_MODEL, layer, lambda i: c, NM_ROWS, lambda i: i)
        return [_prompt_mod_spec(*args), _sample_mod_spec(*args)]

    xs = x if isinstance(x, tuple) else (x,)
    return pl.pallas_call(
        _normmod_kernel,
        out_shape=jax.ShapeDtypeStruct((M_ALL, D_MODEL), BF16),
        grid=(M_ALL // NM_ROWS,),
        in_specs=_x_specs(x, NM_ROWS, D_MODEL, lambda i: i, lambda i: 0)
        + [pl.BlockSpec((1, D_MODEL), lambda i: (0, 0))] + specs(shift_col) + specs(scale_col),
        out_specs=pl.BlockSpec((NM_ROWS, D_MODEL), lambda i: (i, 0)),
        compiler_params=_cparams(1),
        name="norm_modulate",
    )(*xs, g_row, mod4, mod4, mod4, mod4)


def _final_norm_kernel(x_ref, g_ref, o_ref):
    x = x_ref[...]
    o_ref[...] = x * lax.rsqrt(jnp.mean(x * x, axis=-1, keepdims=True) + EPS) * g_ref[...]


def _final_norm(x, g_row, row0, n_rows):
    return pl.pallas_call(
        _final_norm_kernel,
        out_shape=jax.ShapeDtypeStruct((n_rows, D_MODEL), F32),
        grid=(n_rows // NM_ROWS,),
        in_specs=[
            pl.BlockSpec((NM_ROWS, D_MODEL), lambda i: (row0 // NM_ROWS + i, 0)),
            pl.BlockSpec((1, D_MODEL), lambda i: (0, 0)),
        ],
        out_specs=pl.BlockSpec((NM_ROWS, D_MODEL), lambda i: (i, 0)),
        compiler_params=_cparams(1),
        name="final_norm",
    )(x, g_row)


def _cast_weight(w_ref, wb_ref):
    @pl.when(pl.program_id(1) == 0)
    def _():
        wb_ref[...] = w_ref[...].astype(BF16)


def _mm_kernel(a_ref, w_ref, o_ref, wb_ref, *, relu2):
    _cast_weight(w_ref, wb_ref)
    acc = jnp.dot(a_ref[...], wb_ref[...], preferred_element_type=F32)
    if relu2:
        acc = jnp.square(jnp.maximum(acc, 0.0))
    o_ref[...] = acc.astype(o_ref.dtype)


def _matmul(a, w, layer, n_cols, *, relu2, out_dtype):
    k = a.shape[1]
    return pl.pallas_call(
        functools.partial(_mm_kernel, relu2=relu2),
        out_shape=jax.ShapeDtypeStruct((M_ALL, n_cols), out_dtype),
        grid=(n_cols // TN, M_ALL // TM),
        in_specs=[
            pl.BlockSpec((TM, k), lambda j, i: (i, 0)),
            pl.BlockSpec((None, k, TN), lambda j, i: (layer, 0, j)),
        ],
        out_specs=pl.BlockSpec((TM, TN), lambda j, i: (i, j)),
        scratch_shapes=[pltpu.VMEM((k, TN), BF16)],
        compiler_params=_cparams(2),
        name="proj_relu2" if relu2 else "proj",
    )(a, w)


def _mm_res_kernel(*refs):
    a_ref, w_ref, *x_refs, gp_ref, gs_ref, o_ref, wb_ref = refs
    _cast_weight(w_ref, wb_ref)
    acc = jnp.dot(a_ref[...], wb_ref[...], preferred_element_type=F32)
    is_sample = pl.program_id(1) >= M_PROMPT // TM
    gate = _tile_mod(is_sample, gp_ref, gs_ref)
    shape3 = (TM // GROUP, GROUP, TN)
    o_ref[...] = (_x_tile(is_sample, x_refs).reshape(shape3) + gate * acc.reshape(shape3)).reshape(TM, TN)


def _matmul_residual(a, w, w_layer, x, mod4, layer, gate_col, *, kblk=0, kc=D_MODEL):
    nj = D_MODEL // TN
    mod_args = (TN, layer, lambda j, i: gate_col * nj + j, TM, lambda j, i: i)
    xs = x if isinstance(x, tuple) else (x,)
    return pl.pallas_call(
        _mm_res_kernel,
        out_shape=jax.ShapeDtypeStruct((M_ALL, D_MODEL), F32),
        grid=(nj, M_ALL // TM),
        in_specs=[
            pl.BlockSpec((TM, kc), lambda j, i: (i, kblk)),
            pl.BlockSpec((None, kc, TN), lambda j, i: (w_layer, kblk, j)),
        ] + _x_specs(x, TM, TN, lambda j, i: i, lambda j, i: j) + [
            _prompt_mod_spec(*mod_args),
            _sample_mod_spec(*mod_args),
        ],
        out_specs=pl.BlockSpec((TM, TN), lambda j, i: (i, j)),
        scratch_shapes=[pltpu.VMEM((kc, TN), BF16)],
        input_output_aliases={} if isinstance(x, tuple) else {2: 0},
        compiler_params=_cparams(2),
        name="proj_residual",
    )(a, w, *xs, mod4, mod4)


CS_ROWS = 256


def _bg_kernel(a_ref, w_ref, alog_ref, dtb_ref, o_ref):
    lane_w = lax.broadcasted_iota(jnp.int32, w_ref.shape, 1)
    w = jnp.where(lane_w < 2 * B_HEADS, w_ref[...], 0.0).astype(BF16)
    acc = jnp.dot(a_ref[...], w, preferred_element_type=F32)
    lane = lax.broadcasted_iota(jnp.int32, (CS_ROWS, LANES), 1)
    z = acc + dtb_ref[...]
    softplus = jnp.maximum(z, 0.0) + jnp.log1p(jnp.exp(-jnp.abs(z)))
    g = -jnp.exp(alog_ref[...]) * softplus
    is_sample = pl.program_id(0) * TM >= M_PROMPT
    ii = lax.broadcasted_iota(jnp.int32, (CS_ROWS, CS_ROWS), 0)
    jj = lax.broadcasted_iota(jnp.int32, (CS_ROWS, CS_ROWS), 1)
    chunk_i = jnp.where(is_sample, lax.shift_right_logical(ii, 5), lax.shift_right_logical(ii, 6))
    chunk_j = jnp.where(is_sample, lax.shift_right_logical(jj, 5), lax.shift_right_logical(jj, 6))
    ltri = jnp.where(chunk_i == chunk_j, jnp.where(ii >= jj, 1.0, 0.0), 0.0).astype(BF16)
    for r in range(TM // CS_ROWS):
        rows = slice(r * CS_ROWS, (r + 1) * CS_ROWS)
        gc = pltpu.roll(_dot_exact_lhs(ltri, g[rows]), B_HEADS, 1)
        o_ref[rows, :] = jnp.where(lane < B_HEADS, _sigmoid(acc[rows]),
                                   jnp.where(lane < 2 * B_HEADS, g[rows], gc))


def _beta_gate(a, w_in, layer_j, alog_row, dtb_row):
    return pl.pallas_call(
        _bg_kernel,
        out_shape=jax.ShapeDtypeStruct((M_ALL, LANES), F32),
        grid=(M_ALL // TM,),
        in_specs=[
            pl.BlockSpec((TM, D_MODEL), lambda i: (i, 0)),
            pl.BlockSpec((None, D_MODEL, LANES), lambda i: (layer_j, 0, EVEN_MAIN // LANES)),
            pl.BlockSpec((1, LANES), lambda i: (0, 0)),
            pl.BlockSpec((1, LANES), lambda i: (0, 0)),
        ],
        out_specs=pl.BlockSpec((TM, LANES), lambda i: (i, 0)),
        compiler_params=_cparams(1),
        name="beta_gate",
    )(a, w_in, alog_row, dtb_row)


class _Seg:
    def __init__(self, batch, length, row0, step, chunk, pos0, carry=True):
        self.batch, self.length, self.row0, self.step, self.chunk, self.pos0 = batch, length, row0, step, chunk, pos0
        self.nblk = length // step
        self.nch = step // chunk
        self.carry = carry

    def rb(self, b, n):
        return (self.row0 + b * self.length) // self.step + n


SEG_PROMPT = _Seg(BATCH, SEQ, 0, 256, CHUNK, 0)
SAMPLE_SEQS = 8
SEG_SAMPLE_GROUPS = _Seg(DEC_BATCH // SAMPLE_SEQS, SAMPLE_SEQS * DEC_SEQ, M_PROMPT, SAMPLE_SEQS * DEC_SEQ, DEC_SEQ,
                         PAST_LEN, carry=False)


def _chunk_ids(t, chunk, width):
    return lax.shift_right_logical(lax.broadcasted_iota(jnp.int32, (t, width), 0), chunk.bit_length() - 1)


HPA = 4


def _rope(x, cos, sin):
    return x * cos + pltpu.roll(x, A_DK // 2, 1) * sin


def _ret_kernel(*refs, chunk, nch, carry, aliased):
    if aliased:
        refs = refs[:-4] + refs[-3:]
    (q_ref, k_ref, v_ref, g_ref, cos_ref, sin_ref, dm_ref, qd_ref, kd_ref, cd_ref, gn_ref, s0_ref,
     y_ref, st_ref, s_scr) = refs
    n = pl.program_id(2)
    t = chunk * nch
    hs = range(HPA)
    kcols = [slice(hh * A_DK, (hh + 1) * A_DK) for hh in hs]
    vcols = [slice(hh * A_DV, (hh + 1) * A_DV) for hh in hs]

    if carry:
        @pl.when(n == 0)
        def _():
            s_scr[...] = s0_ref[...]

    cos = cos_ref[...]
    sin = sin_ref[...]
    cid_k = _chunk_ids(t, chunk, A_DK)
    cid_v = _chunk_ids(t, chunk, A_DV)

    def per_chunk(x, cid):
        if nch == 1:
            return x
        return jnp.concatenate([jnp.where(cid == c, x, 0.0) for c in range(nch)], axis=1)

    q = [_rope(q_ref[:, kcols[hh]], cos, sin) for hh in hs]
    k = [_rope(k_ref[:, kcols[hh]], cos, sin) * (A_DK ** -0.5) for hh in hs]
    v = [v_ref[:, vcols[hh]] for hh in hs]
    scores = [_dot_nt(q[hh], k[hh]) * dm_ref[hh] for hh in hs]
    kv = [_dot_tn(k[hh] * kd_ref[hh], per_chunk(v[hh], cid_v)) for hh in hs]
    states = []
    for hh in hs:
        cd = cd_ref[hh]
        kv_c = [kv[hh][:, c * A_DV:(c + 1) * A_DV] for c in range(nch)]
        if carry:
            chain = [s_scr[hh]]
            for c in range(nch):
                chain.append(chain[-1] * cd + kv_c[c])
            s_scr[hh] = chain[nch]
            states.append(chain[:nch])
        else:
            states.append([s0_ref[c, hh] for c in range(nch)])
            for c in range(nch):
                st_ref[c, hh] = states[hh][c] * cd + kv_c[c]
    o = [_dot(jnp.concatenate([per_chunk(q[hh] * qd_ref[hh], cid_k), scores[hh]], axis=1),
              jnp.concatenate(states[hh] + [v[hh]], axis=0)) for hh in hs]
    for hh in hs:
        mu = jnp.mean(o[hh], axis=-1, keepdims=True)
        d = o[hh] - mu
        var = jnp.mean(d * d, axis=-1, keepdims=True)
        on = d * lax.rsqrt(var + EPS) * gn_ref[hh]
        y_ref[:, vcols[hh]] = (_silu(g_ref[:, vcols[hh]]) * on).astype(y_ref.dtype)

    if carry:
        @pl.when(n == pl.num_programs(2) - 1)
        def _():
            st_ref[...] = s_scr[...]


def _retention_tables(chunk, nch):
    t = chunk * nch
    lg = np.log1p(-np.exp2(-5.0 - np.arange(A_HEADS, dtype=np.float64)))
    idx = np.arange(t)
    pos = idx % chunk
    rel = pos[:, None] - pos[None, :]
    same = (idx[:, None] // chunk) == (idx[None, :] // chunk)
    dm = np.where((same & (rel >= 0))[None], np.exp(np.maximum(rel, 0)[None] * lg[:, None, None]), 0.0)
    qd = np.exp((pos[None, :] + 1.0) * lg[:, None])
    kd = np.exp((chunk - 1.0 - pos)[None, :] * lg[:, None])
    cd = np.exp(chunk * lg)
    bc = lambda x, w: jnp.asarray(np.broadcast_to(x[:, :, None], x.shape + (w,)), F32)
    return jnp.asarray(dm, F32), bc(qd, A_DK), bc(kd, A_DK), bc(cd[:, None], A_DV)


def _rope_tables(pos0, length, period):
    half = A_DK // 2
    inv_freq = ROPE_BASE ** (-np.arange(half, dtype=np.float64) / half)
    ang = (pos0 + np.arange(length, dtype=np.float64) % period)[:, None] * inv_freq[None, :]
    cos, sin = np.cos(ang), np.sin(ang)
    return (jnp.asarray(np.concatenate([cos, cos], axis=-1), F32),
            jnp.asarray(np.concatenate([-sin, sin], axis=-1), F32))


def _retention(seg, p, y_in, layer_j, gn4, s0):
    t, c = seg.step, seg.chunk
    dm, qd, kd, cd = _retention_tables(c, seg.nch)
    cos, sin = _rope_tables(seg.pos0, seg.length, seg.length if seg.carry else c)
    rb = seg.rb
    kw, vw = HPA * A_DK, HPA * A_DV
    hgroups = A_HEADS // HPA
    k_blk = A_QK // kw
    v_blk = 2 * A_QK // vw
    g_blk = (2 * A_QK + A_V) // vw
    state_spec = pl.BlockSpec((None if seg.carry else seg.nch, HPA, A_DK, A_DV), lambda b, h, n: (b, h, 0, 0))
    in_specs = [
        pl.BlockSpec((t, kw), lambda b, h, n: (rb(b, n), h)),
        pl.BlockSpec((t, kw), lambda b, h, n: (rb(b, n), k_blk + h)),
        pl.BlockSpec((t, vw), lambda b, h, n: (rb(b, n), v_blk + h)),
        pl.BlockSpec((t, vw), lambda b, h, n: (rb(b, n), g_blk + h)),
        pl.BlockSpec((t, A_DK), lambda b, h, n: (n, 0)),
        pl.BlockSpec((t, A_DK), lambda b, h, n: (n, 0)),
        pl.BlockSpec((HPA, t, t), lambda b, h, n: (h, 0, 0)),
        pl.BlockSpec((HPA, t, A_DK), lambda b, h, n: (h, 0, 0)),
        pl.BlockSpec((HPA, t, A_DK), lambda b, h, n: (h, 0, 0)),
        pl.BlockSpec((HPA, 1, A_DV), lambda b, h, n: (h, 0, 0)),
        pl.BlockSpec((None, HPA, 1, A_DV), lambda b, h, n: (layer_j, h, 0, 0)),
        state_spec,
    ]
    args = [p, p, p, p, cos, sin, dm, qd, kd, cd, gn4, s0]
    aliases = {}
    if y_in is not None:
        in_specs.append(pl.BlockSpec(memory_space=pl.ANY))
        args.append(y_in)
        aliases = {len(args) - 1: 0}
    return pl.pallas_call(
        functools.partial(_ret_kernel, chunk=c, nch=seg.nch, carry=seg.carry, aliased=y_in is not None),
        out_shape=(jax.ShapeDtypeStruct((M_ALL, D_MODEL), BF16), jax.ShapeDtypeStruct(s0.shape, F32)),
        grid=(seg.batch, hgroups, seg.nblk),
        in_specs=in_specs,
        out_specs=(pl.BlockSpec((t, vw), lambda b, h, n: (rb(b, n), h)), state_spec),
        scratch_shapes=[pltpu.VMEM((HPA, A_DK, A_DV), F32)],
        input_output_aliases=aliases,
        compiler_params=_cparams(3),
        name="retention",
    )(*args)


HPS = 8


def _unit_lower_solve(a, rhs, same_blk, nblocks):
    hs = range(len(a))
    n = a[0].shape[0]
    d = [jnp.where(same_blk, a[h], 0.0) for h in hs]
    y = [jnp.concatenate([a[h] - d[h], rhs[h]], axis=1) for h in hs]
    sign = -1.0
    for _ in range(BD.bit_length() - 2):
        r = [_dot(d[h], jnp.concatenate([d[h], y[h]], axis=1)) for h in hs]
        d = [r[h][:, :n] for h in hs]
        y = [y[h] + sign * r[h][:, n:] for h in hs]
        sign = 1.0
    y = [y[h] + _dot(d[h], y[h]) for h in hs]
    f = [y[h][:, :n] for h in hs]
    y = [y[h][:, n:] for h in hs]
    if nblocks == 1:
        return y
    if nblocks == 2:
        return [y[h] - _dot(f[h], y[h]) for h in hs]
    r = [_dot(f[h], jnp.concatenate([f[h], y[h]], axis=1)) for h in hs]
    f2 = [r[h][:, :n] for h in hs]
    y = [y[h] - r[h][:, n:] for h in hs]
    return [y[h] + _dot(f2[h], y[h]) for h in hs]


def _delta_kernel(*refs, chunk, nch, carry):
    (q_ref, k_ref, v_ref, z_ref, bg_ref, cwq_ref, cwk_ref, cwv_ref, tq_ref, tk_ref, tv_ref, dn_ref, s0_ref,
     _, y_ref, st_ref, s_scr, tail_scr) = refs
    hg = pl.program_id(1)
    n = pl.program_id(2)
    t = chunk * nch
    log2_chunk = chunk.bit_length() - 1
    tail_refs = (tq_ref, tk_ref, tv_ref)
    hs = range(HPS)
    cols = [slice(hh * B_DK, (hh + 1) * B_DK) for hh in hs]

    if carry:
        @pl.when(n == 0)
        def _():
            s_scr[...] = s0_ref[...]
            for part in range(3):
                tail_scr[part] = tail_refs[part][...]

    ii = lax.broadcasted_iota(jnp.int32, (t, t), 0)
    jj = lax.broadcasted_iota(jnp.int32, (t, t), 1)
    same_chunk = lax.shift_right_logical(ii, log2_chunk) == lax.shift_right_logical(jj, log2_chunk)
    lower = jnp.logical_and(same_chunk, ii >= jj)
    strict = jnp.logical_and(same_chunk, ii > jj)
    same_blk = lax.shift_right_logical(ii, BD.bit_length() - 1) == lax.shift_right_logical(jj, BD.bit_length() - 1)
    chunk_id = _chunk_ids(t, chunk, LANES)
    bg = bg_ref[...]
    lane = lax.broadcasted_iota(jnp.int32, bg.shape, 1)
    dn = dn_ref[...]

    def conv(u_ref, cw_ref, part, cs):
        u = u_ref[:, cs]
        w = cw_ref[:, cs]
        out = u * w[CONV_W - 1:CONV_W, :]
        first = 8 - (CONV_W - 1)
        if carry:
            ext = jnp.concatenate([tail_scr[part, :, cs], u], axis=0)
            for i in range(CONV_W - 1):
                out = out + ext[first + i:first + i + t, :] * w[i:i + 1, :]
            tail_scr[part, :, cs] = u[t - 8:t, :]
        else:
            ext = jnp.concatenate([tail_refs[part][:, :, cs], u.reshape(nch, chunk, B_DK)], axis=1)
            for i in range(CONV_W - 1):
                out = out + (ext[:, first + i:first + i + chunk, :] * w[i:i + 1, :][None]).reshape(t, B_DK)
        return _silu(out)

    cq = [conv(q_ref, cwq_ref, 0, cs) for cs in cols]
    ck = [conv(k_ref, cwk_ref, 1, cs) for cs in cols]
    v = [conv(v_ref, cwv_ref, 2, cs) for cs in cols]
    q = [x * lax.rsqrt(jnp.sum(x * x, axis=-1, keepdims=True) + EPS) * (B_DK ** -0.5) for x in cq]
    k = [x * lax.rsqrt(jnp.sum(x * x, axis=-1, keepdims=True) + EPS) for x in ck]

    def head_column(lane0, hh):
        return jnp.sum(jnp.where(lane == lane0 + hg * HPS + hh, bg, 0.0), axis=-1, keepdims=True)

    beta = [head_column(0, hh) for hh in hs]
    gc = [head_column(2 * B_HEADS, hh) for hh in hs]
    gcb = [jnp.broadcast_to(x, (t, LANES)) for x in gc]

    def decay_mask(hh):
        gct = gcb[hh].T
        row = gct[:t] if t <= LANES else jnp.concatenate([gct] * (t // LANES), axis=0)
        col = jnp.broadcast_to(gc[hh], (t, t))
        return jnp.where(lower, jnp.exp(col - row), 0.0)

    dmask = [decay_mask(hh) for hh in hs]
    kb = [k[hh] * beta[hh] for hh in hs]
    kq = [_dot_nt(jnp.concatenate([kb[hh], q[hh]], axis=0), k[hh]) for hh in hs]
    a = [jnp.where(strict, kq[hh][:t] * dmask[hh], 0.0) for hh in hs]
    qk = [kq[hh][t:] * dmask[hh] for hh in hs]
    eg = [jnp.exp(x) for x in gcb]
    uw = _unit_lower_solve(a, [jnp.concatenate([v[hh] * beta[hh], kb[hh] * eg[hh]], axis=1) for hh in hs],
                           same_blk, chunk // BD)
    u = [x[:, :B_DV] for x in uw]
    w = [x[:, B_DV:] for x in uw]

    qw = [_dot(qk[hh], jnp.concatenate([w[hh], u[hh]], axis=1)) for hh in hs]
    qp = [q[hh] * eg[hh] - qw[hh][:, :B_DV] for hh in hs]
    op = [x[:, B_DV:] for x in qw]
    last = [[gcb[hh][(c + 1) * chunk - 1:(c + 1) * chunk, :] for c in range(nch)] for hh in hs]
    lastb = [jnp.concatenate([jnp.broadcast_to(l, (chunk, LANES)) for l in last[hh]], axis=0) for hh in hs]
    kdec = [k[hh] * jnp.exp(lastb[hh] - gcb[hh]) for hh in hs]
    wu = [jnp.concatenate([jnp.where(chunk_id == c, w[hh], 0.0) for c in range(nch)]
                          + [jnp.where(chunk_id == c, u[hh], 0.0) for c in range(nch)], axis=1) for hh in hs]
    xn = [_dot_tn(kdec[hh], wu[hh]) for hh in hs]

    s = [s_scr[hh] if carry else None for hh in hs]
    outs = [[] for _ in hs]
    for c in range(nch):
        rows = slice(c * chunk, (c + 1) * chunk)
        if not carry:
            s = [s0_ref[c, hh] for hh in hs]
        r = [_dot(jnp.concatenate([xn[hh][:, c * B_DV:(c + 1) * B_DV], qp[hh][rows]], axis=0), s[hh]) for hh in hs]
        for hh in hs:
            outs[hh].append(r[hh][B_DK:] + op[hh][rows])
        s = [s[hh] * jnp.exp(last[hh][c]) - r[hh][:B_DK] + xn[hh][:, (nch + c) * B_DV:(nch + c + 1) * B_DV]
             for hh in hs]
        if not carry:
            for hh in hs:
                st_ref[c, hh] = s[hh]
    for hh in hs:
        if carry:
            s_scr[hh] = s[hh]
        o = jnp.concatenate(outs[hh], axis=0) if nch > 1 else outs[hh][0]
        on = o * lax.rsqrt(jnp.mean(o * o, axis=-1, keepdims=True) + EPS) * dn
        y_ref[:, cols[hh]] = (on * _silu(z_ref[:, cols[hh]])).astype(y_ref.dtype)

    if carry:
        @pl.when(n == pl.num_programs(2) - 1)
        def _():
            st_ref[...] = s_scr[...]


def _gated_delta(seg, p, bg, y_in, layer_j, conv_w, tails, dn3, s0):
    t, c = seg.step, seg.chunk
    rb = seg.rb
    wide = HPS * B_DK
    q_blk = (2 * A_QK + 2 * A_V) // wide
    z_blk = (2 * A_QK + 2 * A_V + CONV_CH) // wide
    y_blk = A_V // wide
    hgroups = B_HEADS // HPS
    seqs = None if seg.carry else seg.nch

    def pcol(off):
        return pl.BlockSpec((t, wide), lambda b, h, n: (rb(b, n), off + h))

    def wcol(part):
        return pl.BlockSpec((None, CONV_W, wide), lambda b, h, n: (layer_j, 0, part * hgroups + h))

    def tcol(part):
        return pl.BlockSpec((seqs, 8, wide), lambda b, h, n: (b, 0, part * hgroups + h))

    state_spec = pl.BlockSpec((seqs, HPS, B_DK, B_DV), lambda b, h, n: (b, h, 0, 0))
    in_specs = [
        pcol(q_blk), pcol(q_blk + hgroups), pcol(q_blk + 2 * hgroups), pcol(z_blk),
        pl.BlockSpec((t, LANES), lambda b, h, n: (rb(b, n), 0)),
        wcol(0), wcol(1), wcol(2), tcol(0), tcol(1), tcol(2),
        pl.BlockSpec((None, 1, B_DV), lambda b, h, n: (layer_j, 0, 0)),
        state_spec,
        pl.BlockSpec(memory_space=pl.ANY),
    ]
    return pl.pallas_call(
        functools.partial(_delta_kernel, chunk=c, nch=seg.nch, carry=seg.carry),
        out_shape=(jax.ShapeDtypeStruct((M_ALL, D_MODEL), BF16), jax.ShapeDtypeStruct(s0.shape, F32)),
        grid=(seg.batch, hgroups, seg.nblk),
        in_specs=in_specs,
        out_specs=(pl.BlockSpec((t, wide), lambda b, h, n: (rb(b, n), y_blk + h)), state_spec),
        scratch_shapes=[pltpu.VMEM((HPS, B_DK, B_DV), F32), pltpu.VMEM((3, 8, wide), F32)],
        input_output_aliases={13: 0},
        compiler_params=_cparams(3),
        name="gated_delta",
    )(p, p, p, p, bg, conv_w, conv_w, conv_w, tails, tails, tails, dn3, s0, y_in)


TQ = 256
BAND = BAND_PREV * CHUNK
WIN = BAND + TQ
NEG = -1e30
ATTN_UNROLL = 7


def _softmax_pv(s, v_ones):
    p = jnp.exp(s - jnp.max(s, axis=-1, keepdims=True))
    o = _dot(p, v_ones)
    return o[:, :C_HD] / o[:, C_HD:]


BIAS_LANES = 1024


def _bias_vectors(rel_bias, cols, base):
    s = np.arange(BIAS_LANES)
    d = np.where(s < cols, s, s - BIAS_LANES)
    return rel_bias[np.clip(base - d, -REL_CLIP, REL_CLIP) + REL_CLIP].T[:, None, :]


def _toeplitz(vec, rows, cols):
    x = jnp.broadcast_to(vec, (rows, BIAS_LANES))
    return pltpu.roll(x, 0, 1, stride=1, stride_axis=0)[:, :cols]


def _attn_prompt_kernel(q_ref, k_ref, v_ref, vec_ref, o_ref, kp_scr, vp_scr, bias_scr):
    kp_scr[0:BAND, :] = jnp.zeros((BAND, C_HD), BF16)
    vp_scr[0:BAND, 0:C_HD] = jnp.zeros((BAND, C_HD), BF16)
    kp_scr[BAND:, :] = k_ref[...].astype(BF16)
    vp_scr[BAND:, 0:C_HD] = v_ref[...].astype(BF16)
    vp_scr[:, C_HD:] = jnp.ones((BAND + SEQ, C_HD), BF16)
    r = lax.broadcasted_iota(jnp.int32, (TQ, WIN), 0)
    w = lax.broadcasted_iota(jnp.int32, (TQ, WIN), 1)
    j = w - lax.shift_left(lax.shift_right_logical(r, CHUNK.bit_length() - 1), CHUNK.bit_length() - 1)
    inside = jnp.logical_and(j >= 0, j < BAND + CHUNK)
    bias_scr[...] = jnp.where(inside, _toeplitz(vec_ref[...], TQ, WIN), NEG)

    def block(n, masked):
        start = n * TQ if masked else pl.multiple_of(n * TQ, TQ)
        rows = pl.ds(start, TQ)
        s = _dot_nt(q_ref[rows, :] * (C_HD ** -0.5), kp_scr[pl.ds(start, WIN), :]) + bias_scr[...]
        if masked:
            col = lax.broadcasted_iota(jnp.int32, s.shape, 1)
            s = jnp.where(col + start >= BAND, s, NEG)
        o_ref[rows, :] = _softmax_pv(s, vp_scr[pl.ds(start, WIN), :]).astype(o_ref.dtype)

    n_masked = BAND // TQ
    for n in range(n_masked):
        block(n, True)

    def body(n, carry):
        block(n, False)
        return carry

    lax.fori_loop(n_masked, SEQ // TQ, body, 0, unroll=ATTN_UNROLL)


def _attention_prompt(p, rel_bias):
    k_blk = C_WIDTH // C_HD
    return pl.pallas_call(
        _attn_prompt_kernel,
        out_shape=jax.ShapeDtypeStruct((M_ALL, C_WIDTH), BF16),
        grid=(BATCH, C_HEADS),
        in_specs=[
            pl.BlockSpec((SEQ, C_HD), lambda b, h: (b, h)),
            pl.BlockSpec((SEQ, C_HD), lambda b, h: (b, k_blk + h)),
            pl.BlockSpec((SEQ, C_HD), lambda b, h: (b, 2 * k_blk + h)),
            pl.BlockSpec((None, 1, BIAS_LANES), lambda b, h: (h, 0, 0)),
        ],
        out_specs=pl.BlockSpec((SEQ, C_HD), lambda b, h: (b, h)),
        scratch_shapes=[pltpu.VMEM((BAND + SEQ, C_HD), BF16), pltpu.VMEM((BAND + SEQ, 2 * C_HD), BF16),
                        pltpu.VMEM((TQ, WIN), F32)],
        compiler_params=_cparams(2),
        name="band_attention_prompt",
    )(p, p, p, _bias_vectors(rel_bias, WIN, BAND))


HB = 8


def _band_state_kernel(k_ref, v_ref, ko_ref, vo_ref):
    for hh in range(HB):
        cs = slice(hh * C_HD, (hh + 1) * C_HD)
        ko_ref[:, hh, :] = k_ref[:, cs]
        vo_ref[:, hh, :] = v_ref[:, cs]


def _band_state(p):
    wide = HB * C_HD
    k_blk = C_WIDTH // wide
    last = SEQ // BAND - 1
    state = jax.ShapeDtypeStruct((BATCH, BAND, C_HEADS, C_HD), F32)
    state_spec = pl.BlockSpec((None, BAND, HB, C_HD), lambda b, g: (b, 0, g, 0))
    return pl.pallas_call(
        _band_state_kernel,
        out_shape=(state, state),
        grid=(BATCH, C_HEADS // HB),
        in_specs=[
            pl.BlockSpec((BAND, wide), lambda b, g: (b * (SEQ // BAND) + last, k_blk + g)),
            pl.BlockSpec((BAND, wide), lambda b, g: (b * (SEQ // BAND) + last, 2 * k_blk + g)),
        ],
        out_specs=(state_spec, state_spec),
        compiler_params=_cparams(2),
        name="band_state",
    )(p, p)


def _attn_sample_kernel(q_ref, kn_ref, vn_ref, ck_ref, cv_ref, vec_ref, _, o_ref, ko_ref, vo_ref):
    lc = ck_ref.shape[0]
    hs = range(HB)
    cols = [slice(hh * C_HD, (hh + 1) * C_HD) for hh in hs]
    kn = [kn_ref[:, cs] for cs in cols]
    vn = [vn_ref[:, cs] for cs in cols]
    for hh in hs:
        ko_ref[:, hh, :] = kn[hh]
        vo_ref[:, hh, :] = vn[hh]
    kk = [jnp.concatenate([ck_ref[:, hh, :], kn[hh]], axis=0) for hh in hs]
    vv = [jnp.concatenate([cv_ref[:, hh, :], vn[hh]], axis=0) for hh in hs]
    s = [_dot_nt(q_ref[:, cols[hh]] * (C_HD ** -0.5), kk[hh]) + _toeplitz(vec_ref[hh], DEC_SEQ, lc + DEC_SEQ)
         for hh in hs]
    m = [jnp.max(x, axis=-1, keepdims=True) for x in s]
    e = [jnp.exp(s[hh] - m[hh]) for hh in hs]
    l = [jnp.sum(x, axis=-1, keepdims=True) for x in e]
    o = [_dot(e[hh], vv[hh]) for hh in hs]
    for hh in hs:
        o_ref[:, cols[hh]] = (o[hh] / l[hh]).astype(o_ref.dtype)


def _attention_sample(p, o_in, cache_k, cache_v, rel_bias):
    lc = cache_k.shape[1]
    rb0 = M_PROMPT // DEC_SEQ
    wide = HB * C_HD
    k_blk = C_WIDTH // wide
    state = jax.ShapeDtypeStruct((DEC_BATCH, DEC_SEQ, C_HEADS, C_HD), F32)
    state_spec = pl.BlockSpec((None, DEC_SEQ, HB, C_HD), lambda b, g: (b, 0, g, 0))
    cache_spec = pl.BlockSpec((None, lc, HB, C_HD), lambda b, g: (b, 0, g, 0))
    return pl.pallas_call(
        _attn_sample_kernel,
        out_shape=(jax.ShapeDtypeStruct((M_ALL, C_WIDTH), BF16), state, state),
        grid=(DEC_BATCH, C_HEADS // HB),
        in_specs=[
            pl.BlockSpec((DEC_SEQ, wide), lambda b, g: (rb0 + b, g)),
            pl.BlockSpec((DEC_SEQ, wide), lambda b, g: (rb0 + b, k_blk + g)),
            pl.BlockSpec((DEC_SEQ, wide), lambda b, g: (rb0 + b, 2 * k_blk + g)),
            cache_spec, cache_spec,
            pl.BlockSpec((HB, 1, BIAS_LANES), lambda b, g: (g, 0, 0)),
            pl.BlockSpec(memory_space=pl.ANY),
        ],
        out_specs=(pl.BlockSpec((DEC_SEQ, wide), lambda b, g: (rb0 + b, g)), state_spec, state_spec),
        input_output_aliases={6: 0},
        compiler_params=_cparams(2),
        name="band_attention_sample",
    )(p, p, p, cache_k, cache_v, _bias_vectors(rel_bias, lc + DEC_SEQ, lc), o_in)


def _conv_tail(state_conv):
    return jnp.pad(state_conv, ((0, 0), (8 - (CONV_W - 1), 0), (0, 0)))


def _lane_row(x, lanes):
    row = jnp.zeros((LANES,), F32)
    for l0 in lanes:
        row = row.at[l0:l0 + B_HEADS].set(x)
    return row[None, :]


def kernel(x_prompt, x_sample, c_prompt, c_sample, state_ret_0, state_delta_0, state_conv_0, cache_k_1, cache_v_1, state_ret_2, state_delta_2, state_conv_2, cache_k_3, cache_v_3, norm_mix_g, norm_ffn_g, w_mod, b_mod, w_in_even, w_out_even, a_norm_g, b_conv_w, b_a_log, b_dt_bias, b_norm_g, w_in_odd, w_out_odd, c_rel_bias, w_ff1, w_ff2, final_norm_g):
    x = (x_prompt.reshape(M_PROMPT, D_MODEL), x_sample.reshape(M_SAMPLE, D_MODEL))
    c_all = jnp.concatenate([c_sample, c_prompt, jnp.zeros((N_SEQ_PAD - N_SEQ, D_MODEL), F32)], axis=0)
    mod4 = _modulation(c_all, w_mod, b_mod).reshape(DEPTH, N_SEQ_PAD, 1, 6 * D_MODEL)

    sample_states = {0: (state_ret_0, state_delta_0, state_conv_0), 1: (cache_k_1, cache_v_1),
                     2: (state_ret_2, state_delta_2, state_conv_2), 3: (cache_k_3, cache_v_3)}
    gn4 = a_norm_g.reshape(-1, A_HEADS, 1, A_DV)
    dn3 = b_norm_g.reshape(-1, 1, B_DV)
    zeros_ret = jnp.zeros((BATCH, A_HEADS, A_DK, A_DV), F32)
    zeros_delta = jnp.zeros((BATCH, B_HEADS, B_DK, B_DV), F32)
    zeros_tail = jnp.zeros((BATCH, 8, CONV_CH), F32)

    p_states, s_states = [], []
    for l in range(DEPTH):
        j = l // 2
        h = _norm_modulate(x, norm_mix_g[l][None, :], mod4, l, 0, 1)
        if l % 2 == 0:
            p = _matmul(h, w_in_even, j, EVEN_MAIN, relu2=False, out_dtype=F32)
            bg = _beta_gate(h, w_in_even, j, _lane_row(b_a_log[j], (B_HEADS,)), _lane_row(b_dt_bias[j], (B_HEADS,)))
            s_ret, s_delta, s_conv = sample_states[l]
            y, p_ret = _retention(SEG_PROMPT, p, None, j, gn4, zeros_ret)
            y, s_ret_new = _retention(SEG_SAMPLE_GROUPS, p, y, j, gn4, s_ret)
            y, p_delta = _gated_delta(SEG_PROMPT, p, bg, y, j, b_conv_w, zeros_tail, dn3, zeros_delta)
            y, s_delta_new = _gated_delta(SEG_SAMPLE_GROUPS, p, bg, y, j, b_conv_w, _conv_tail(s_conv), dn3, s_delta)
            c0 = 2 * A_QK + 2 * A_V
            p_conv = jnp.stack([lax.slice(p, ((b + 1) * SEQ - (CONV_W - 1), c0), ((b + 1) * SEQ, c0 + CONV_CH))
                                for b in range(BATCH)])
            s_conv_new = lax.slice(p, (M_PROMPT, c0), (M_ALL, c0 + CONV_CH)).reshape(
                DEC_BATCH, DEC_SEQ, CONV_CH)[:, DEC_SEQ - (CONV_W - 1):]
            p_states.append((p_ret, p_delta, p_conv))
            s_states.append((s_ret_new, s_delta_new, s_conv_new))
            x = _matmul_residual(y, w_out_even, j, x, mod4, l, 2)
        else:
            p = _matmul(h, w_in_odd, j, 3 * C_WIDTH, relu2=False, out_dtype=F32)
            cache_k, cache_v = sample_states[l]
            o = _attention_prompt(p, c_rel_bias[j])
            o, sk, sv = _attention_sample(p, o, cache_k, cache_v, c_rel_bias[j])
            p_states.append(_band_state(p))
            s_states.append((sk, sv))
            x = _matmul_residual(o, w_out_odd, j, x, mod4, l, 2)
        h = _norm_modulate(x, norm_ffn_g[l][None, :], mod4, l, 3, 4)
        f = _matmul(h, w_ff1, l, D_FF, relu2=True, out_dtype=BF16)
        for kblk in range(D_FF // D_MODEL):
            x = _matmul_residual(f, w_ff2, l, x, mod4, l, 5, kblk=kblk)

    g_row = final_norm_g[None, :]
    y_prompt = _final_norm(x, g_row, 0, M_PROMPT).reshape(BATCH, SEQ, D_MODEL)
    y_sample = _final_norm(x, g_row, M_PROMPT, M_SAMPLE).reshape(DEC_BATCH, DEC_SEQ, D_MODEL)
    (p_ret_0, p_delta_0, p_conv_0), (p_k_1, p_v_1), (p_ret_2, p_delta_2, p_conv_2), (p_k_3, p_v_3) = p_states
    (s_ret_0, s_delta_0, s_conv_0), (s_k_1, s_v_1), (s_ret_2, s_delta_2, s_conv_2), (s_k_3, s_v_3) = s_states
    return (y_prompt, y_sample,
            p_ret_0, s_ret_0, p_delta_0, s_delta_0, p_conv_0, s_conv_0,
            p_k_1, s_k_1, p_v_1, s_v_1,
            p_ret_2, s_ret_2, p_delta_2, s_delta_2, p_conv_2, s_conv_2,
            p_k_3, s_k_3, p_v_3, s_v_3)
```

```python
import functools

import numpy as np
import jax
import jax.numpy as jnp
from jax import lax
from jax.experimental import pallas as pl
from jax.experimental.pallas import tpu as pltpu

F32 = jnp.float32
BF16 = jnp.bfloat16

D_MODEL = 4096
BATCH = 4
SEQ = 4096
DEPTH = 4
DEC_BATCH = 32
DEC_SEQ = 32
PAST_LEN = 1024
CHUNK = 64
EPS = 1e-6
A_DK = 128
A_DV = 256
A_HEADS = 8
A_QK = A_HEADS * A_DK
A_V = A_HEADS * A_DV
ROPE_BASE = 10000.0
B_DK = 128
B_DV = 128
B_HEADS = 16
B_QK = B_HEADS * B_DK
B_V = B_HEADS * B_DV
CONV_W = 4
CONV_CH = 2 * B_QK + B_V
EVEN_MAIN = 2 * A_QK + 2 * A_V + CONV_CH + B_V
EVEN_IN = EVEN_MAIN + 2 * B_HEADS
C_HD = 128
C_HEADS = 32
C_WIDTH = C_HEADS * C_HD
BAND_PREV = 8
REL_CLIP = 128
D_FF = 4 * D_MODEL

GROUP = 32
M_PROMPT = BATCH * SEQ
M_SAMPLE = DEC_BATCH * DEC_SEQ
M_ALL = M_PROMPT + M_SAMPLE
N_SEQ = BATCH + DEC_BATCH
N_SEQ_PAD = 40

VMEM_LIMIT = 56 * 1024 * 1024
TM = 1024
TN = 512
LANES = 128
BD = 16


def _cparams(n_axes):
    return pltpu.CompilerParams(dimension_semantics=("arbitrary",) * n_axes, vmem_limit_bytes=VMEM_LIMIT)


def _sigmoid(x):
    return 1.0 / (1.0 + jnp.exp(-x))


def _silu(x):
    return x * _sigmoid(x)


def _dot(a, b):
    return jnp.dot(a.astype(BF16), b.astype(BF16), preferred_element_type=F32)


def _dot_nt(a, b):
    return lax.dot_general(a.astype(BF16), b.astype(BF16), (((1,), (1,)), ((), ())), preferred_element_type=F32)


def _dot_tn(a, b):
    return lax.dot_general(a.astype(BF16), b.astype(BF16), (((0,), (0,)), ((), ())), preferred_element_type=F32)


def _dot_exact_lhs(l_bf16, x):
    d = functools.partial(jnp.dot, preferred_element_type=F32)
    x0 = x.astype(BF16)
    r1 = x - x0.astype(F32)
    x1 = r1.astype(BF16)
    x2 = (r1 - x1.astype(F32)).astype(BF16)
    return d(l_bf16, x0) + (d(l_bf16, x1) + d(l_bf16, x2))


def _mod_kernel(c_ref, w_ref, b_ref, o_ref):
    c = c_ref[...]
    o_ref[...] = _dot(_silu(c), w_ref[...]) + b_ref[...]


def _modulation(c_all, w_mod, b_mod):
    n_out = 6 * D_MODEL
    return pl.pallas_call(
        _mod_kernel,
        out_shape=jax.ShapeDtypeStruct((DEPTH, N_SEQ_PAD, n_out), F32),
        grid=(DEPTH, n_out // TN),
        in_specs=[
            pl.BlockSpec((N_SEQ_PAD, D_MODEL), lambda l, j: (0, 0)),
            pl.BlockSpec((None, D_MODEL, TN), lambda l, j: (l, 0, j)),
            pl.BlockSpec((None, 1, TN), lambda l, j: (l, 0, j)),
        ],
        out_specs=pl.BlockSpec((None, N_SEQ_PAD, TN), lambda l, j: (l, 0, j)),
        compiler_params=_cparams(2),
        name="modulation",
    )(c_all, w_mod, b_mod.reshape(DEPTH, 1, n_out))


def _prompt_mod_spec(width, layer, col, rows_per_tile, tile_of):
    tiles_per_seq = SEQ // rows_per_tile
    return pl.BlockSpec(
        (None, None, 1, width),
        lambda *g: (layer, DEC_BATCH + jnp.minimum(tile_of(*g) // tiles_per_seq, BATCH - 1), 0, col(*g)))


def _sample_mod_spec(width, layer, col, rows_per_tile, tile_of):
    seqs = rows_per_tile // GROUP
    first = M_PROMPT // rows_per_tile
    return pl.BlockSpec(
        (None, seqs, 1, width),
        lambda *g: (layer, jnp.maximum(tile_of(*g) - first, 0), 0, col(*g)))


def _tile_mod(is_sample, gp_ref, gs_ref):
    return jnp.where(is_sample, gs_ref[...], gp_ref[...][None])


NM_ROWS = 512


def _x_specs(x, rows, cols, tile_of, col_of):
    if not isinstance(x, tuple):
        return [pl.BlockSpec((rows, cols), lambda *g: (tile_of(*g), col_of(*g)))]
    last_prompt = M_PROMPT // rows - 1
    return [pl.BlockSpec((rows, cols), lambda *g: (jnp.minimum(tile_of(*g), last_prompt), col_of(*g))),
            pl.BlockSpec((rows, cols), lambda *g: (jnp.maximum(tile_of(*g) - last_prompt - 1, 0), col_of(*g)))]


def _x_tile(is_sample, x_refs):
    if len(x_refs) == 1:
        return x_refs[0][...]
    return jnp.where(is_sample, x_refs[1][...], x_refs[0][...])


def _normmod_kernel(*refs):
    *x_refs, g_ref, shp_ref, shs_ref, scp_ref, scs_ref, o_ref = refs
    is_sample = pl.program_id(0) >= M_PROMPT // NM_ROWS
    x = _x_tile(is_sample, x_refs)
    y = x * lax.rsqrt(jnp.mean(x * x, axis=-1, keepdims=True) + EPS) * g_ref[...]
    y3 = y.reshape(NM_ROWS // GROUP, GROUP, D_MODEL)
    h = y3 * (1.0 + _tile_mod(is_sample, scp_ref, scs_ref)) + _tile_mod(is_sample, shp_ref, shs_ref)
    o_ref[...] = h.reshape(NM_ROWS, D_MODEL).astype(o_ref.dtype)


def _norm_modulate(x, g_row, mod4, layer, shift_col, scale_col):
    def specs(c):
        args = (D_MODEL, layer, lambda i: c, NM_ROWS, lambda i: i)
        return [_prompt_mod_spec(*args), _sample_mod_spec(*args)]

    xs = x if isinstance(x, tuple) else (x,)
    return pl.pallas_call(
        _normmod_kernel,
        out_shape=jax.ShapeDtypeStruct((M_ALL, D_MODEL), BF16),
        grid=(M_ALL // NM_ROWS,),
        in_specs=_x_specs(x, NM_ROWS, D_MODEL, lambda i: i, lambda i: 0)
        + [pl.BlockSpec((1, D_MODEL), lambda i: (0, 0))] + specs(shift_col) + specs(scale_col),
        out_specs=pl.BlockSpec((NM_ROWS, D_MODEL), lambda i: (i, 0)),
        compiler_params=_cparams(1),
        name="norm_modulate",
    )(*xs, g_row, mod4, mod4, mod4, mod4)


def _final_norm_kernel(x_ref, g_ref, o_ref):
    x = x_ref[...]
    o_ref[...] = x * lax.rsqrt(jnp.mean(x * x, axis=-1, keepdims=True) + EPS) * g_ref[...]


def _final_norm(x, g_row, row0, n_rows):
    return pl.pallas_call(
        _final_norm_kernel,
        out_shape=jax.ShapeDtypeStruct((n_rows, D_MODEL), F32),
        grid=(n_rows // NM_ROWS,),
        in_specs=[
            pl.BlockSpec((NM_ROWS, D_MODEL), lambda i: (row0 // NM_ROWS + i, 0)),
            pl.BlockSpec((1, D_MODEL), lambda i: (0, 0)),
        ],
        out_specs=pl.BlockSpec((NM_ROWS, D_MODEL), lambda i: (i, 0)),
        compiler_params=_cparams(1),
        name="final_norm",
    )(x, g_row)


def _cast_weight(w_ref, wb_ref):
    @pl.when(pl.program_id(1) == 0)
    def _():
        wb_ref[...] = w_ref[...].astype(BF16)


def _mm_kernel(a_ref, w_ref, o_ref, wb_ref, *, relu2, w_is_nk):
    _cast_weight(w_ref, wb_ref)
    if w_is_nk:
        acc = lax.dot_general(a_ref[...], wb_ref[...], (((1,), (1,)), ((), ())), preferred_element_type=F32)
    else:
        acc = jnp.dot(a_ref[...], wb_ref[...], preferred_element_type=F32)
    if relu2:
        acc = jnp.square(jnp.maximum(acc, 0.0))
    o_ref[...] = acc.astype(o_ref.dtype)


def _matmul(a, w, layer, n_cols, *, relu2, out_dtype, w_is_nk=False):
    k = a.shape[1]
    if w_is_nk:
        w_spec, w_block = pl.BlockSpec((None, TN, k), lambda j, i: (layer, j, 0)), (TN, k)
    else:
        w_spec, w_block = pl.BlockSpec((None, k, TN), lambda j, i: (layer, 0, j)), (k, TN)
    return pl.pallas_call(
        functools.partial(_mm_kernel, relu2=relu2, w_is_nk=w_is_nk),
        out_shape=jax.ShapeDtypeStruct((M_ALL, n_cols), out_dtype),
        grid=(n_cols // TN, M_ALL // TM),
        in_specs=[pl.BlockSpec((TM, k), lambda j, i: (i, 0)), w_spec],
        out_specs=pl.BlockSpec((TM, TN), lambda j, i: (i, j)),
        scratch_shapes=[pltpu.VMEM(w_block, BF16)],
        compiler_params=_cparams(2),
        name="proj_relu2" if relu2 else "proj",
    )(a, w)


def _mm_res_kernel(*refs):
    a_ref, w_ref, *x_refs, gp_ref, gs_ref, o_ref, wb_ref = refs
    _cast_weight(w_ref, wb_ref)
    acc = jnp.dot(a_ref[...], wb_ref[...], preferred_element_type=F32)
    is_sample = pl.program_id(1) >= M_PROMPT // TM
    gate = _tile_mod(is_sample, gp_ref, gs_ref)
    shape3 = (TM // GROUP, GROUP, TN)
    o_ref[...] = (_x_tile(is_sample, x_refs).reshape(shape3) + gate * acc.reshape(shape3)).reshape(TM, TN)


def _matmul_residual(a, w, w_layer, x, mod4, layer, gate_col, *, kblk=0, kc=D_MODEL):
    nj = D_MODEL // TN
    mod_args = (TN, layer, lambda j, i: gate_col * nj + j, TM, lambda j, i: i)
    xs = x if isinstance(x, tuple) else (x,)
    return pl.pallas_call(
        _mm_res_kernel,
        out_shape=jax.ShapeDtypeStruct((M_ALL, D_MODEL), F32),
        grid=(nj, M_ALL // TM),
        in_specs=[
            pl.BlockSpec((TM, kc), lambda j, i: (i, kblk)),
            pl.BlockSpec((None, kc, TN), lambda j, i: (w_layer, kblk, j)),
        ] + _x_specs(x, TM, TN, lambda j, i: i, lambda j, i: j) + [
            _prompt_mod_spec(*mod_args),
            _sample_mod_spec(*mod_args),
        ],
        out_specs=pl.BlockSpec((TM, TN), lambda j, i: (i, j)),
        scratch_shapes=[pltpu.VMEM((kc, TN), BF16)],
        input_output_aliases={} if isinstance(x, tuple) else {2: 0},
        compiler_params=_cparams(2),
        name="proj_residual",
    )(a, w, *xs, mod4, mod4)


CS_ROWS = 256


def _bg_kernel(a_ref, w_ref, alog_ref, dtb_ref, o_ref):
    row_w = lax.broadcasted_iota(jnp.int32, w_ref.shape, 0)
    w = jnp.where(row_w < 2 * B_HEADS, w_ref[...], 0.0).astype(BF16)
    acc = lax.dot_general(a_ref[...], w, (((1,), (1,)), ((), ())), preferred_element_type=F32)
    lane = lax.broadcasted_iota(jnp.int32, (CS_ROWS, LANES), 1)
    z = acc + dtb_ref[...]
    softplus = jnp.maximum(z, 0.0) + jnp.log1p(jnp.exp(-jnp.abs(z)))
    g = -jnp.exp(alog_ref[...]) * softplus
    is_sample = pl.program_id(0) * TM >= M_PROMPT
    ii = lax.broadcasted_iota(jnp.int32, (CS_ROWS, CS_ROWS), 0)
    jj = lax.broadcasted_iota(jnp.int32, (CS_ROWS, CS_ROWS), 1)
    chunk_i = jnp.where(is_sample, lax.shift_right_logical(ii, 5), lax.shift_right_logical(ii, 6))
    chunk_j = jnp.where(is_sample, lax.shift_right_logical(jj, 5), lax.shift_right_logical(jj, 6))
    ltri = jnp.where(chunk_i == chunk_j, jnp.where(ii >= jj, 1.0, 0.0), 0.0).astype(BF16)
    for r in range(TM // CS_ROWS):
        rows = slice(r * CS_ROWS, (r + 1) * CS_ROWS)
        gc = pltpu.roll(_dot_exact_lhs(ltri, g[rows]), B_HEADS, 1)
        o_ref[rows, :] = jnp.where(lane < B_HEADS, _sigmoid(acc[rows]),
                                   jnp.where(lane < 2 * B_HEADS, g[rows], gc))


def _beta_gate(a, w_in_t, layer_j, alog_row, dtb_row):
    return pl.pallas_call(
        _bg_kernel,
        out_shape=jax.ShapeDtypeStruct((M_ALL, LANES), F32),
        grid=(M_ALL // TM,),
        in_specs=[
            pl.BlockSpec((TM, D_MODEL), lambda i: (i, 0)),
            pl.BlockSpec((None, LANES, D_MODEL), lambda i: (layer_j, EVEN_MAIN // LANES, 0)),
            pl.BlockSpec((1, LANES), lambda i: (0, 0)),
            pl.BlockSpec((1, LANES), lambda i: (0, 0)),
        ],
        out_specs=pl.BlockSpec((TM, LANES), lambda i: (i, 0)),
        compiler_params=_cparams(1),
        name="beta_gate",
    )(a, w_in_t, alog_row, dtb_row)


class _Seg:
    def __init__(self, batch, length, row0, step, chunk, pos0, carry=True):
        self.batch, self.length, self.row0, self.step, self.chunk, self.pos0 = batch, length, row0, step, chunk, pos0
        self.nblk = length // step
        self.nch = step // chunk
        self.carry = carry

    def rb(self, b, n):
        return (self.row0 + b * self.length) // self.step + n


SEG_PROMPT = _Seg(BATCH, SEQ, 0, 256, CHUNK, 0)
SAMPLE_SEQS = 8
SEG_SAMPLE_GROUPS = _Seg(DEC_BATCH // SAMPLE_SEQS, SAMPLE_SEQS * DEC_SEQ, M_PROMPT, SAMPLE_SEQS * DEC_SEQ, DEC_SEQ,
                         PAST_LEN, carry=False)


def _chunk_ids(t, chunk, width):
    return lax.shift_right_logical(lax.broadcasted_iota(jnp.int32, (t, width), 0), chunk.bit_length() - 1)


HPA = 4


def _rope(x, cos, sin):
    return x * cos + pltpu.roll(x, A_DK // 2, 1) * sin


def _ret_kernel(*refs, chunk, nch, carry, aliased):
    if aliased:
        refs = refs[:-4] + refs[-3:]
    (q_ref, k_ref, v_ref, g_ref, cos_ref, sin_ref, dm_ref, qd_ref, kd_ref, cd_ref, gn_ref, s0_ref,
     y_ref, st_ref, s_scr) = refs
    n = pl.program_id(2)
    t = chunk * nch
    hs = range(HPA)
    kcols = [slice(hh * A_DK, (hh + 1) * A_DK) for hh in hs]
    vcols = [slice(hh * A_DV, (hh + 1) * A_DV) for hh in hs]

    if carry:
        @pl.when(n == 0)
        def _():
            s_scr[...] = s0_ref[...]

    cos = cos_ref[...]
    sin = sin_ref[...]
    cid_k = _chunk_ids(t, chunk, A_DK)
    cid_v = _chunk_ids(t, chunk, A_DV)

    def per_chunk(x, cid):
        if nch == 1:
            return x
        return jnp.concatenate([jnp.where(cid == c, x, 0.0) for c in range(nch)], axis=1)

    q = [_rope(q_ref[:, kcols[hh]], cos, sin) for hh in hs]
    k = [_rope(k_ref[:, kcols[hh]], cos, sin) * (A_DK ** -0.5) for hh in hs]
    v = [v_ref[:, vcols[hh]] for hh in hs]
    scores = [_dot_nt(q[hh], k[hh]) * dm_ref[hh] for hh in hs]
    kv = [_dot_tn(k[hh] * kd_ref[hh], per_chunk(v[hh], cid_v)) for hh in hs]
    states = []
    for hh in hs:
        cd = cd_ref[hh]
        kv_c = [kv[hh][:, c * A_DV:(c + 1) * A_DV] for c in range(nch)]
        if carry:
            chain = [s_scr[hh]]
            for c in range(nch):
                chain.append(chain[-1] * cd + kv_c[c])
            s_scr[hh] = chain[nch]
            states.append(chain[:nch])
        else:
            states.append([s0_ref[c, hh] for c in range(nch)])
            for c in range(nch):
                st_ref[c, hh] = states[hh][c] * cd + kv_c[c]
    o = [_dot(jnp.concatenate([per_chunk(q[hh] * qd_ref[hh], cid_k), scores[hh]], axis=1),
              jnp.concatenate(states[hh] + [v[hh]], axis=0)) for hh in hs]
    for hh in hs:
        mu = jnp.mean(o[hh], axis=-1, keepdims=True)
        d = o[hh] - mu
        var = jnp.mean(d * d, axis=-1, keepdims=True)
        on = d * lax.rsqrt(var + EPS) * gn_ref[hh]
        y_ref[:, vcols[hh]] = (_silu(g_ref[:, vcols[hh]]) * on).astype(y_ref.dtype)

    if carry:
        @pl.when(n == pl.num_programs(2) - 1)
        def _():
            st_ref[...] = s_scr[...]


def _retention_tables(chunk, nch):
    t = chunk * nch
    lg = np.log1p(-np.exp2(-5.0 - np.arange(A_HEADS, dtype=np.float64)))
    idx = np.arange(t)
    pos = idx % chunk
    rel = pos[:, None] - pos[None, :]
    same = (idx[:, None] // chunk) == (idx[None, :] // chunk)
    dm = np.where((same & (rel >= 0))[None], np.exp(np.maximum(rel, 0)[None] * lg[:, None, None]), 0.0)
    qd = np.exp((pos[None, :] + 1.0) * lg[:, None])
    kd = np.exp((chunk - 1.0 - pos)[None, :] * lg[:, None])
    cd = np.exp(chunk * lg)
    bc = lambda x, w: jnp.asarray(np.broadcast_to(x[:, :, None], x.shape + (w,)), F32)
    return jnp.asarray(dm, F32), bc(qd, A_DK), bc(kd, A_DK), bc(cd[:, None], A_DV)


def _rope_tables(pos0, length, period):
    half = A_DK // 2
    inv_freq = ROPE_BASE ** (-np.arange(half, dtype=np.float64) / half)
    ang = (pos0 + np.arange(length, dtype=np.float64) % period)[:, None] * inv_freq[None, :]
    cos, sin = np.cos(ang), np.sin(ang)
    return (jnp.asarray(np.concatenate([cos, cos], axis=-1), F32),
            jnp.asarray(np.concatenate([-sin, sin], axis=-1), F32))


def _retention(seg, p, y_in, layer_j, gn4, s0):
    t, c = seg.step, seg.chunk
    dm, qd, kd, cd = _retention_tables(c, seg.nch)
    cos, sin = _rope_tables(seg.pos0, seg.length, seg.length if seg.carry else c)
    rb = seg.rb
    kw, vw = HPA * A_DK, HPA * A_DV
    hgroups = A_HEADS // HPA
    k_blk = A_QK // kw
    v_blk = 2 * A_QK // vw
    g_blk = (2 * A_QK + A_V) // vw
    state_spec = pl.BlockSpec((None if seg.carry else seg.nch, HPA, A_DK, A_DV), lambda b, h, n: (b, h, 0, 0))
    in_specs = [
        pl.BlockSpec((t, kw), lambda b, h, n: (rb(b, n), h)),
        pl.BlockSpec((t, kw), lambda b, h, n: (rb(b, n), k_blk + h)),
        pl.BlockSpec((t, vw), lambda b, h, n: (rb(b, n), v_blk + h)),
        pl.BlockSpec((t, vw), lambda b, h, n: (rb(b, n), g_blk + h)),
        pl.BlockSpec((t, A_DK), lambda b, h, n: (n, 0)),
        pl.BlockSpec((t, A_DK), lambda b, h, n: (n, 0)),
        pl.BlockSpec((HPA, t, t), lambda b, h, n: (h, 0, 0)),
        pl.BlockSpec((HPA, t, A_DK), lambda b, h, n: (h, 0, 0)),
        pl.BlockSpec((HPA, t, A_DK), lambda b, h, n: (h, 0, 0)),
        pl.BlockSpec((HPA, 1, A_DV), lambda b, h, n: (h, 0, 0)),
        pl.BlockSpec((None, HPA, 1, A_DV), lambda b, h, n: (layer_j, h, 0, 0)),
        state_spec,
    ]
    args = [p, p, p, p, cos, sin, dm, qd, kd, cd, gn4, s0]
    aliases = {}
    if y_in is not None:
        in_specs.append(pl.BlockSpec(memory_space=pl.ANY))
        args.append(y_in)
        aliases = {len(args) - 1: 0}
    return pl.pallas_call(
        functools.partial(_ret_kernel, chunk=c, nch=seg.nch, carry=seg.carry, aliased=y_in is not None),
        out_shape=(jax.ShapeDtypeStruct((M_ALL, D_MODEL), BF16), jax.ShapeDtypeStruct(s0.shape, F32)),
        grid=(seg.batch, hgroups, seg.nblk),
        in_specs=in_specs,
        out_specs=(pl.BlockSpec((t, vw), lambda b, h, n: (rb(b, n), h)), state_spec),
        scratch_shapes=[pltpu.VMEM((HPA, A_DK, A_DV), F32)],
        input_output_aliases=aliases,
        compiler_params=_cparams(3),
        name="retention",
    )(*args)


HPS = 8


def _unit_lower_solve(a, rhs, same_blk, nblocks):
    hs = range(len(a))
    n = a[0].shape[0]
    d = [jnp.where(same_blk, a[h], 0.0) for h in hs]
    y = [jnp.concatenate([a[h] - d[h], rhs[h]], axis=1) for h in hs]
    sign = -1.0
    for _ in range(BD.bit_length() - 2):
        r = [_dot(d[h], jnp.concatenate([d[h], y[h]], axis=1)) for h in hs]
        d = [r[h][:, :n] for h in hs]
        y = [y[h] + sign * r[h][:, n:] for h in hs]
        sign = 1.0
    y = [y[h] + _dot(d[h], y[h]) for h in hs]
    f = [y[h][:, :n] for h in hs]
    y = [y[h][:, n:] for h in hs]
    if nblocks == 1:
        return y
    if nblocks == 2:
        return [y[h] - _dot(f[h], y[h]) for h in hs]
    r = [_dot(f[h], jnp.concatenate([f[h], y[h]], axis=1)) for h in hs]
    f2 = [r[h][:, :n] for h in hs]
    y = [y[h] - r[h][:, n:] for h in hs]
    return [y[h] + _dot(f2[h], y[h]) for h in hs]


def _delta_kernel(*refs, chunk, nch, carry):
    (q_ref, k_ref, v_ref, z_ref, bg_ref, cwq_ref, cwk_ref, cwv_ref, tq_ref, tk_ref, tv_ref, dn_ref, s0_ref,
     _, y_ref, st_ref, s_scr, tail_scr) = refs
    hg = pl.program_id(1)
    n = pl.program_id(2)
    t = chunk * nch
    log2_chunk = chunk.bit_length() - 1
    tail_refs = (tq_ref, tk_ref, tv_ref)
    hs = range(HPS)
    cols = [slice(hh * B_DK, (hh + 1) * B_DK) for hh in hs]

    if carry:
        @pl.when(n == 0)
        def _():
            s_scr[...] = s0_ref[...]
            for part in range(3):
                tail_scr[part] = tail_refs[part][...]

    ii = lax.broadcasted_iota(jnp.int32, (t, t), 0)
    jj = lax.broadcasted_iota(jnp.int32, (t, t), 1)
    same_chunk = lax.shift_right_logical(ii, log2_chunk) == lax.shift_right_logical(jj, log2_chunk)
    lower = jnp.logical_and(same_chunk, ii >= jj)
    strict = jnp.logical_and(same_chunk, ii > jj)
    same_blk = lax.shift_right_logical(ii, BD.bit_length() - 1) == lax.shift_right_logical(jj, BD.bit_length() - 1)
    chunk_id = _chunk_ids(t, chunk, LANES)
    bg = bg_ref[...]
    lane = lax.broadcasted_iota(jnp.int32, bg.shape, 1)
    dn = dn_ref[...]

    def conv(u_ref, cw_ref, part, cs):
        u = u_ref[:, cs]
        w = cw_ref[:, cs]
        out = u * w[CONV_W - 1:CONV_W, :]
        first = 8 - (CONV_W - 1)
        if carry:
            ext = jnp.concatenate([tail_scr[part, :, cs], u], axis=0)
            for i in range(CONV_W - 1):
                out = out + ext[first + i:first + i + t, :] * w[i:i + 1, :]
            tail_scr[part, :, cs] = u[t - 8:t, :]
        else:
            ext = jnp.concatenate([tail_refs[part][:, :, cs], u.reshape(nch, chunk, B_DK)], axis=1)
            for i in range(CONV_W - 1):
                out = out + (ext[:, first + i:first + i + chunk, :] * w[i:i + 1, :][None]).reshape(t, B_DK)
        return _silu(out)

    cq = [conv(q_ref, cwq_ref, 0, cs) for cs in cols]
    ck = [conv(k_ref, cwk_ref, 1, cs) for cs in cols]
    v = [conv(v_ref, cwv_ref, 2, cs) for cs in cols]
    q = [x * lax.rsqrt(jnp.sum(x * x, axis=-1, keepdims=True) + EPS) * (B_DK ** -0.5) for x in cq]
    k = [x * lax.rsqrt(jnp.sum(x * x, axis=-1, keepdims=True) + EPS) for x in ck]

    def head_column(lane0, hh):
        return jnp.sum(jnp.where(lane == lane0 + hg * HPS + hh, bg, 0.0), axis=-1, keepdims=True)

    beta = [head_column(0, hh) for hh in hs]
    gc = [head_column(2 * B_HEADS, hh) for hh in hs]
    gcb = [jnp.broadcast_to(x, (t, LANES)) for x in gc]

    def decay_mask(hh):
        gct = gcb[hh].T
        row = gct[:t] if t <= LANES else jnp.concatenate([gct] * (t // LANES), axis=0)
        col = jnp.broadcast_to(gc[hh], (t, t))
        return jnp.where(lower, jnp.exp(col - row), 0.0)

    dmask = [decay_mask(hh) for hh in hs]
    kb = [k[hh] * beta[hh] for hh in hs]
    kq = [_dot_nt(jnp.concatenate([kb[hh], q[hh]], axis=0), k[hh]) for hh in hs]
    a = [jnp.where(strict, kq[hh][:t] * dmask[hh], 0.0) for hh in hs]
    qk = [kq[hh][t:] * dmask[hh] for hh in hs]
    eg = [jnp.exp(x) for x in gcb]
    uw = _unit_lower_solve(a, [jnp.concatenate([v[hh] * beta[hh], kb[hh] * eg[hh]], axis=1) for hh in hs],
                           same_blk, chunk // BD)
    u = [x[:, :B_DV] for x in uw]
    w = [x[:, B_DV:] for x in uw]

    qw = [_dot(qk[hh], jnp.concatenate([w[hh], u[hh]], axis=1)) for hh in hs]
    qp = [q[hh] * eg[hh] - qw[hh][:, :B_DV] for hh in hs]
    op = [x[:, B_DV:] for x in qw]
    last = [[gcb[hh][(c + 1) * chunk - 1:(c + 1) * chunk, :] for c in range(nch)] for hh in hs]
    lastb = [jnp.concatenate([jnp.broadcast_to(l, (chunk, LANES)) for l in last[hh]], axis=0) for hh in hs]
    kdec = [k[hh] * jnp.exp(lastb[hh] - gcb[hh]) for hh in hs]
    wu = [jnp.concatenate([jnp.where(chunk_id == c, w[hh], 0.0) for c in range(nch)]
                          + [jnp.where(chunk_id == c, u[hh], 0.0) for c in range(nch)], axis=1) for hh in hs]
    xn = [_dot_tn(kdec[hh], wu[hh]) for hh in hs]

    s = [s_scr[hh] if carry else None for hh in hs]
    outs = [[] for _ in hs]
    for c in range(nch):
        rows = slice(c * chunk, (c + 1) * chunk)
        if not carry:
            s = [s0_ref[c, hh] for hh in hs]
        r = [_dot(jnp.concatenate([xn[hh][:, c * B_DV:(c + 1) * B_DV], qp[hh][rows]], axis=0), s[hh]) for hh in hs]
        for hh in hs:
            outs[hh].append(r[hh][B_DK:] + op[hh][rows])
        s = [s[hh] * jnp.exp(last[hh][c]) - r[hh][:B_DK] + xn[hh][:, (nch + c) * B_DV:(nch + c + 1) * B_DV]
             for hh in hs]
        if not carry:
            for hh in hs:
                st_ref[c, hh] = s[hh]
    for hh in hs:
        if carry:
            s_scr[hh] = s[hh]
        o = jnp.concatenate(outs[hh], axis=0) if nch > 1 else outs[hh][0]
        on = o * lax.rsqrt(jnp.mean(o * o, axis=-1, keepdims=True) + EPS) * dn
        y_ref[:, cols[hh]] = (on * _silu(z_ref[:, cols[hh]])).astype(y_ref.dtype)

    if carry:
        @pl.when(n == pl.num_programs(2) - 1)
        def _():
            st_ref[...] = s_scr[...]


def _gated_delta(seg, p, bg, y_in, layer_j, conv_w, tails, dn3, s0):
    t, c = seg.step, seg.chunk
    rb = seg.rb
    wide = HPS * B_DK
    q_blk = (2 * A_QK + 2 * A_V) // wide
    z_blk = (2 * A_QK + 2 * A_V + CONV_CH) // wide
    y_blk = A_V // wide
    hgroups = B_HEADS // HPS
    seqs = None if seg.carry else seg.nch

    def pcol(off):
        return pl.BlockSpec((t, wide), lambda b, h, n: (rb(b, n), off + h))

    def wcol(part):
        return pl.BlockSpec((None, CONV_W, wide), lambda b, h, n: (layer_j, 0, part * hgroups + h))

    def tcol(part):
        return pl.BlockSpec((seqs, 8, wide), lambda b, h, n: (b, 0, part * hgroups + h))

    state_spec = pl.BlockSpec((seqs, HPS, B_DK, B_DV), lambda b, h, n: (b, h, 0, 0))
    in_specs = [
        pcol(q_blk), pcol(q_blk + hgroups), pcol(q_blk + 2 * hgroups), pcol(z_blk),
        pl.BlockSpec((t, LANES), lambda b, h, n: (rb(b, n), 0)),
        wcol(0), wcol(1), wcol(2), tcol(0), tcol(1), tcol(2),
        pl.BlockSpec((None, 1, B_DV), lambda b, h, n: (layer_j, 0, 0)),
        state_spec,
        pl.BlockSpec(memory_space=pl.ANY),
    ]
    return pl.pallas_call(
        functools.partial(_delta_kernel, chunk=c, nch=seg.nch, carry=seg.carry),
        out_shape=(jax.ShapeDtypeStruct((M_ALL, D_MODEL), BF16), jax.ShapeDtypeStruct(s0.shape, F32)),
        grid=(seg.batch, hgroups, seg.nblk),
        in_specs=in_specs,
        out_specs=(pl.BlockSpec((t, wide), lambda b, h, n: (rb(b, n), y_blk + h)), state_spec),
        scratch_shapes=[pltpu.VMEM((HPS, B_DK, B_DV), F32), pltpu.VMEM((3, 8, wide), F32)],
        input_output_aliases={13: 0},
        compiler_params=_cparams(3),
        name="gated_delta",
    )(p, p, p, p, bg, conv_w, conv_w, conv_w, tails, tails, tails, dn3, s0, y_in)


TQ = 256
BAND = BAND_PREV * CHUNK
WIN = BAND + TQ
NEG = -1e30
ATTN_UNROLL = 14


def _softmax_pv(s, v_ones):
    p = jnp.exp(s - jnp.max(s, axis=-1, keepdims=True))
    o = _dot(p, v_ones)
    return o[:, :C_HD] / o[:, C_HD:]


BIAS_LANES = 1024


def _bias_vectors(rel_bias, cols, base):
    s = np.arange(BIAS_LANES)
    d = np.where(s < cols, s, s - BIAS_LANES)
    return rel_bias[np.clip(base - d, -REL_CLIP, REL_CLIP) + REL_CLIP].T[:, None, :]


def _toeplitz(vec, rows, cols):
    x = jnp.broadcast_to(vec, (rows, BIAS_LANES))
    return pltpu.roll(x, 0, 1, stride=1, stride_axis=0)[:, :cols]


def _attn_prompt_kernel(q_ref, k_ref, v_ref, vec_ref, o_ref, kp_scr, vp_scr, bias_scr):
    kp_scr[0:BAND, :] = jnp.zeros((BAND, C_HD), BF16)
    vp_scr[0:BAND, 0:C_HD] = jnp.zeros((BAND, C_HD), BF16)
    kp_scr[BAND:, :] = k_ref[...].astype(BF16)
    vp_scr[BAND:, 0:C_HD] = v_ref[...].astype(BF16)
    vp_scr[:, C_HD:] = jnp.ones((BAND + SEQ, C_HD), BF16)
    r = lax.broadcasted_iota(jnp.int32, (TQ, WIN), 0)
    w = lax.broadcasted_iota(jnp.int32, (TQ, WIN), 1)
    j = w - lax.shift_left(lax.shift_right_logical(r, CHUNK.bit_length() - 1), CHUNK.bit_length() - 1)
    inside = jnp.logical_and(j >= 0, j < BAND + CHUNK)
    bias_scr[...] = jnp.where(inside, _toeplitz(vec_ref[...], TQ, WIN), NEG)

    def block(n, masked):
        start = n * TQ if masked else pl.multiple_of(n * TQ, TQ)
        rows = pl.ds(start, TQ)
        s = _dot_nt(q_ref[rows, :] * (C_HD ** -0.5), kp_scr[pl.ds(start, WIN), :]) + bias_scr[...]
        if masked:
            col = lax.broadcasted_iota(jnp.int32, s.shape, 1)
            s = jnp.where(col + start >= BAND, s, NEG)
        o_ref[rows, :] = _softmax_pv(s, vp_scr[pl.ds(start, WIN), :]).astype(o_ref.dtype)

    n_masked = BAND // TQ
    for n in range(n_masked):
        block(n, True)

    def body(n, carry):
        block(n, False)
        return carry

    lax.fori_loop(n_masked, SEQ // TQ, body, 0, unroll=ATTN_UNROLL)


def _attention_prompt(p, rel_bias):
    k_blk = C_WIDTH // C_HD
    return pl.pallas_call(
        _attn_prompt_kernel,
        out_shape=jax.ShapeDtypeStruct((M_ALL, C_WIDTH), BF16),
        grid=(BATCH, C_HEADS),
        in_specs=[
            pl.BlockSpec((SEQ, C_HD), lambda b, h: (b, h)),
            pl.BlockSpec((SEQ, C_HD), lambda b, h: (b, k_blk + h)),
            pl.BlockSpec((SEQ, C_HD), lambda b, h: (b, 2 * k_blk + h)),
            pl.BlockSpec((None, 1, BIAS_LANES), lambda b, h: (h, 0, 0)),
        ],
        out_specs=pl.BlockSpec((SEQ, C_HD), lambda b, h: (b, h)),
        scratch_shapes=[pltpu.VMEM((BAND + SEQ, C_HD), BF16), pltpu.VMEM((BAND + SEQ, 2 * C_HD), BF16),
                        pltpu.VMEM((TQ, WIN), F32)],
        compiler_params=_cparams(2),
        name="band_attention_prompt",
    )(p, p, p, _bias_vectors(rel_bias, WIN, BAND))


HB = 8


def _band_state_kernel(k_ref, v_ref, ko_ref, vo_ref):
    for hh in range(HB):
        cs = slice(hh * C_HD, (hh + 1) * C_HD)
        ko_ref[:, hh, :] = k_ref[:, cs]
        vo_ref[:, hh, :] = v_ref[:, cs]


def _band_state(p):
    wide = HB * C_HD
    k_blk = C_WIDTH // wide
    last = SEQ // BAND - 1
    state = jax.ShapeDtypeStruct((BATCH, BAND, C_HEADS, C_HD), F32)
    state_spec = pl.BlockSpec((None, BAND, HB, C_HD), lambda b, g: (b, 0, g, 0))
    return pl.pallas_call(
        _band_state_kernel,
        out_shape=(state, state),
        grid=(BATCH, C_HEADS // HB),
        in_specs=[
            pl.BlockSpec((BAND, wide), lambda b, g: (b * (SEQ // BAND) + last, k_blk + g)),
            pl.BlockSpec((BAND, wide), lambda b, g: (b * (SEQ // BAND) + last, 2 * k_blk + g)),
        ],
        out_specs=(state_spec, state_spec),
        compiler_params=_cparams(2),
        name="band_state",
    )(p, p)


def _attn_sample_kernel(q_ref, kn_ref, vn_ref, ck_ref, cv_ref, vec_ref, _, o_ref, ko_ref, vo_ref):
    lc = ck_ref.shape[0]
    hs = range(HB)
    cols = [slice(hh * C_HD, (hh + 1) * C_HD) for hh in hs]
    kn = [kn_ref[:, cs] for cs in cols]
    vn = [vn_ref[:, cs] for cs in cols]
    for hh in hs:
        ko_ref[:, hh, :] = kn[hh]
        vo_ref[:, hh, :] = vn[hh]
    kk = [jnp.concatenate([ck_ref[:, hh, :], kn[hh]], axis=0) for hh in hs]
    vv = [jnp.concatenate([cv_ref[:, hh, :], vn[hh]], axis=0) for hh in hs]
    s = [_dot_nt(q_ref[:, cols[hh]] * (C_HD ** -0.5), kk[hh]) + _toeplitz(vec_ref[hh], DEC_SEQ, lc + DEC_SEQ)
         for hh in hs]
    m = [jnp.max(x, axis=-1, keepdims=True) for x in s]
    e = [jnp.exp(s[hh] - m[hh]) for hh in hs]
    l = [jnp.sum(x, axis=-1, keepdims=True) for x in e]
    o = [_dot(e[hh], vv[hh]) for hh in hs]
    for hh in hs:
        o_ref[:, cols[hh]] = (o[hh] / l[hh]).astype(o_ref.dtype)


def _attention_sample(p, o_in, cache_k, cache_v, rel_bias):
    lc = cache_k.shape[1]
    rb0 = M_PROMPT // DEC_SEQ
    wide = HB * C_HD
    k_blk = C_WIDTH // wide
    state = jax.ShapeDtypeStruct((DEC_BATCH, DEC_SEQ, C_HEADS, C_HD), F32)
    state_spec = pl.BlockSpec((None, DEC_SEQ, HB, C_HD), lambda b, g: (b, 0, g, 0))
    cache_spec = pl.BlockSpec((None, lc, HB, C_HD), lambda b, g: (b, 0, g, 0))
    return pl.pallas_call(
        _attn_sample_kernel,
        out_shape=(jax.ShapeDtypeStruct((M_ALL, C_WIDTH), BF16), state, state),
        grid=(DEC_BATCH, C_HEADS // HB),
        in_specs=[
            pl.BlockSpec((DEC_SEQ, wide), lambda b, g: (rb0 + b, g)),
            pl.BlockSpec((DEC_SEQ, wide), lambda b, g: (rb0 + b, k_blk + g)),
            pl.BlockSpec((DEC_SEQ, wide), lambda b, g: (rb0 + b, 2 * k_blk + g)),
            cache_spec, cache_spec,
            pl.BlockSpec((HB, 1, BIAS_LANES), lambda b, g: (g, 0, 0)),
            pl.BlockSpec(memory_space=pl.ANY),
        ],
        out_specs=(pl.BlockSpec((DEC_SEQ, wide), lambda b, g: (rb0 + b, g)), state_spec, state_spec),
        input_output_aliases={6: 0},
        compiler_params=_cparams(2),
        name="band_attention_sample",
    )(p, p, p, cache_k, cache_v, _bias_vectors(rel_bias, lc + DEC_SEQ, lc), o_in)


def _conv_tail(state_conv):
    return jnp.pad(state_conv, ((0, 0), (8 - (CONV_W - 1), 0), (0, 0)))


def _lane_row(x, lanes):
    row = jnp.zeros((LANES,), F32)
    for l0 in lanes:
        row = row.at[l0:l0 + B_HEADS].set(x)
    return row[None, :]


def kernel(x_prompt, x_sample, c_prompt, c_sample, state_ret_0, state_delta_0, state_conv_0, cache_k_1, cache_v_1, state_ret_2, state_delta_2, state_conv_2, cache_k_3, cache_v_3, norm_mix_g, norm_ffn_g, w_mod, b_mod, w_in_even, w_out_even, a_norm_g, b_conv_w, b_a_log, b_dt_bias, b_norm_g, w_in_odd, w_out_odd, c_rel_bias, w_ff1, w_ff2, final_norm_g):
    x = (x_prompt.reshape(M_PROMPT, D_MODEL), x_sample.reshape(M_SAMPLE, D_MODEL))
    c_all = jnp.concatenate([c_sample, c_prompt, jnp.zeros((N_SEQ_PAD - N_SEQ, D_MODEL), F32)], axis=0)
    mod4 = _modulation(c_all, w_mod, b_mod).reshape(DEPTH, N_SEQ_PAD, 1, 6 * D_MODEL)

    sample_states = {0: (state_ret_0, state_delta_0, state_conv_0), 1: (cache_k_1, cache_v_1),
                     2: (state_ret_2, state_delta_2, state_conv_2), 3: (cache_k_3, cache_v_3)}
    w_in_even_t = jnp.swapaxes(w_in_even, 1, 2)
    gn4 = a_norm_g.reshape(-1, A_HEADS, 1, A_DV)
    dn3 = b_norm_g.reshape(-1, 1, B_DV)
    zeros_ret = jnp.zeros((BATCH, A_HEADS, A_DK, A_DV), F32)
    zeros_delta = jnp.zeros((BATCH, B_HEADS, B_DK, B_DV), F32)
    zeros_tail = jnp.zeros((BATCH, 8, CONV_CH), F32)

    p_states, s_states = [], []
    for l in range(DEPTH):
        j = l // 2
        h = _norm_modulate(x, norm_mix_g[l][None, :], mod4, l, 0, 1)
        if l % 2 == 0:
            p = _matmul(h, w_in_even_t, j, EVEN_MAIN, relu2=False, out_dtype=F32, w_is_nk=True)
            bg = _beta_gate(h, w_in_even_t, j, _lane_row(b_a_log[j], (B_HEADS,)), _lane_row(b_dt_bias[j], (B_HEADS,)))
            s_ret, s_delta, s_conv = sample_states[l]
            y, p_ret = _retention(SEG_PROMPT, p, None, j, gn4, zeros_ret)
            y, s_ret_new = _retention(SEG_SAMPLE_GROUPS, p, y, j, gn4, s_ret)
            y, p_delta = _gated_delta(SEG_PROMPT, p, bg, y, j, b_conv_w, zeros_tail, dn3, zeros_delta)
            y, s_delta_new = _gated_delta(SEG_SAMPLE_GROUPS, p, bg, y, j, b_conv_w, _conv_tail(s_conv), dn3, s_delta)
            c0 = 2 * A_QK + 2 * A_V
            p_conv = jnp.stack([lax.slice(p, ((b + 1) * SEQ - (CONV_W - 1), c0), ((b + 1) * SEQ, c0 + CONV_CH))
                                for b in range(BATCH)])
            s_conv_new = lax.slice(p, (M_PROMPT, c0), (M_ALL, c0 + CONV_CH)).reshape(
                DEC_BATCH, DEC_SEQ, CONV_CH)[:, DEC_SEQ - (CONV_W - 1):]
            p_states.append((p_ret, p_delta, p_conv))
            s_states.append((s_ret_new, s_delta_new, s_conv_new))
            x = _matmul_residual(y, w_out_even, j, x, mod4, l, 2)
        else:
            p = _matmul(h, w_in_odd, j, 3 * C_WIDTH, relu2=False, out_dtype=F32)
            cache_k, cache_v = sample_states[l]
            o = _attention_prompt(p, c_rel_bias[j])
            o, sk, sv = _attention_sample(p, o, cache_k, cache_v, c_rel_bias[j])
            p_states.append(_band_state(p))
            s_states.append((sk, sv))
            x = _matmul_residual(o, w_out_odd, j, x, mod4, l, 2)
        h = _norm_modulate(x, norm_ffn_g[l][None, :], mod4, l, 3, 4)
        f = _matmul(h, w_ff1, l, D_FF, relu2=True, out_dtype=BF16)
        for kblk in range(D_FF // D_MODEL):
            x = _matmul_residual(f, w_ff2, l, x, mod4, l, 5, kblk=kblk)

    g_row = final_norm_g[None, :]
    y_prompt = _final_norm(x, g_row, 0, M_PROMPT).reshape(BATCH, SEQ, D_MODEL)
    y_sample = _final_norm(x, g_row, M_PROMPT, M_SAMPLE).reshape(DEC_BATCH, DEC_SEQ, D_MODEL)
    (p_ret_0, p_delta_0, p_conv_0), (p_k_1, p_v_1), (p_ret_2, p_delta_2, p_conv_2), (p_k_3, p_v_3) = p_states
    (s_ret_0, s_delta_0, s_conv_0), (s_k_1, s_v_1), (s_ret_2, s_delta_2, s_conv_2), (s_k_3, s_v_3) = s_states
    return (y_prompt, y_sample,
            p_ret_0, s_ret_0, p_delta_0, s_delta_0, p_conv_0, s_conv_0,
            p_k_1, s_k_1, p_v_1, s_v_1,
            p_ret_2, s_ret_2, p_delta_2, s_delta_2, p_conv_2, s_conv_2,
            p_k_3, s_k_3, p_v_3, s_v_3)
```

```python
import functools

import numpy as np
import jax
import jax.numpy as jnp
from jax import lax
from jax.experimental import pallas as pl
from jax.experimental.pallas import tpu as pltpu

F32 = jnp.float32
BF16 = jnp.bfloat16

D_MODEL = 4096
BATCH = 4
SEQ = 4096
DEPTH = 4
DEC_BATCH = 32
DEC_SEQ = 32
PAST_LEN = 1024
CHUNK = 64
EPS = 1e-6
A_DK = 128
A_DV = 256
A_HEADS = 8
A_QK = A_HEADS * A_DK
A_V = A_HEADS * A_DV
ROPE_BASE = 10000.0
B_DK = 128
B_DV = 128
B_HEADS = 16
B_QK = B_HEADS * B_DK
B_V = B_HEADS * B_DV
CONV_W = 4
CONV_CH = 2 * B_QK + B_V
EVEN_MAIN = 2 * A_QK + 2 * A_V + CONV_CH + B_V
EVEN_IN = EVEN_MAIN + 2 * B_HEADS
C_HD = 128
C_HEADS = 32
C_WIDTH = C_HEADS * C_HD
BAND_PREV = 8
REL_CLIP = 128
D_FF = 4 * D_MODEL

GROUP = 32
M_PROMPT = BATCH * SEQ
M_SAMPLE = DEC_BATCH * DEC_SEQ
M_ALL = M_PROMPT + M_SAMPLE
N_SEQ = BATCH + DEC_BATCH
N_SEQ_PAD = 40

VMEM_LIMIT = 56 * 1024 * 1024
TM = 1024
TN = 512
LANES = 128
BD = 16


def _cparams(n_axes):
    return pltpu.CompilerParams(dimension_semantics=("arbitrary",) * n_axes, vmem_limit_bytes=VMEM_LIMIT)


def _sigmoid(x):
    return 1.0 / (1.0 + jnp.exp(-x))


def _silu(x):
    return x * _sigmoid(x)


def _dot(a, b):
    return jnp.dot(a.astype(BF16), b.astype(BF16), preferred_element_type=F32)


def _dot_nt(a, b):
    return lax.dot_general(a.astype(BF16), b.astype(BF16), (((1,), (1,)), ((), ())), preferred_element_type=F32)


def _dot_tn(a, b):
    return lax.dot_general(a.astype(BF16), b.astype(BF16), (((0,), (0,)), ((), ())), preferred_element_type=F32)


def _dot_exact_lhs(l_bf16, x):
    d = functools.partial(jnp.dot, preferred_element_type=F32)
    x0 = x.astype(BF16)
    r1 = x - x0.astype(F32)
    x1 = r1.astype(BF16)
    x2 = (r1 - x1.astype(F32)).astype(BF16)
    return d(l_bf16, x0) + (d(l_bf16, x1) + d(l_bf16, x2))


def _mod_kernel(c_ref, w_ref, b_ref, o_ref):
    c = c_ref[...]
    o_ref[...] = _dot(_silu(c), w_ref[...]) + b_ref[...]


def _modulation(c_all, w_mod, b_mod):
    n_out = 6 * D_MODEL
    return pl.pallas_call(
        _mod_kernel,
        out_shape=jax.ShapeDtypeStruct((DEPTH, N_SEQ_PAD, n_out), F32),
        grid=(DEPTH, n_out // TN),
        in_specs=[
            pl.BlockSpec((N_SEQ_PAD, D_MODEL), lambda l, j: (0, 0)),
            pl.BlockSpec((None, D_MODEL, TN), lambda l, j: (l, 0, j)),
            pl.BlockSpec((None, 1, TN), lambda l, j: (l, 0, j)),
        ],
        out_specs=pl.BlockSpec((None, N_SEQ_PAD, TN), lambda l, j: (l, 0, j)),
        compiler_params=_cparams(2),
        name="modulation",
    )(c_all, w_mod, b_mod.reshape(DEPTH, 1, n_out))


def _mod_spec(width, layer, col):
    return pl.BlockSpec((None, N_SEQ_PAD, 1, width), lambda *g: (layer, 0, 0, col(*g)))


def _tile_mod(tile, rows_per_tile, g_ref):
    seqs = rows_per_tile // GROUP
    first_sample = M_PROMPT // rows_per_tile
    prompt_row = DEC_BATCH + jnp.minimum(tile // (SEQ // rows_per_tile), BATCH - 1)
    sample_row = pl.multiple_of(jnp.maximum(tile - first_sample, 0) * seqs, seqs)
    return jnp.where(tile >= first_sample, g_ref[pl.ds(sample_row, seqs)], g_ref[pl.ds(prompt_row, 1)])


NM_ROWS = 512


def _x_specs(x, rows, cols, tile_of, col_of):
    if not isinstance(x, tuple):
        return [pl.BlockSpec((rows, cols), lambda *g: (tile_of(*g), col_of(*g)))]
    last_prompt = M_PROMPT // rows - 1
    return [pl.BlockSpec((rows, cols), lambda *g: (jnp.minimum(tile_of(*g), last_prompt), col_of(*g))),
            pl.BlockSpec((rows, cols), lambda *g: (jnp.maximum(tile_of(*g) - last_prompt - 1, 0), col_of(*g)))]


def _x_tile(is_sample, x_refs):
    if len(x_refs) == 1:
        return x_refs[0][...]
    return jnp.where(is_sample, x_refs[1][...], x_refs[0][...])


def _normmod_kernel(*refs):
    *x_refs, g_ref, sh_ref, sc_ref, o_ref = refs
    tile = pl.program_id(0)
    x = _x_tile(tile >= M_PROMPT // NM_ROWS, x_refs)
    y = x * lax.rsqrt(jnp.mean(x * x, axis=-1, keepdims=True) + EPS) * g_ref[...]
    y3 = y.reshape(NM_ROWS // GROUP, GROUP, D_MODEL)
    h = y3 * (1.0 + _tile_mod(tile, NM_ROWS, sc_ref)) + _tile_mod(tile, NM_ROWS, sh_ref)
    o_ref[...] = h.reshape(NM_ROWS, D_MODEL).astype(o_ref.dtype)


def _norm_modulate(x, g_row, mod4, layer, shift_col, scale_col):
    xs = x if isinstance(x, tuple) else (x,)
    return pl.pallas_call(
        _normmod_kernel,
        out_shape=jax.ShapeDtypeStruct((M_ALL, D_MODEL), BF16),
        grid=(M_ALL // NM_ROWS,),
        in_specs=_x_specs(x, NM_ROWS, D_MODEL, lambda i: i, lambda i: 0)
        + [pl.BlockSpec((1, D_MODEL), lambda i: (0, 0)),
           _mod_spec(D_MODEL, layer, lambda i: shift_col), _mod_spec(D_MODEL, layer, lambda i: scale_col)],
        out_specs=pl.BlockSpec((NM_ROWS, D_MODEL), lambda i: (i, 0)),
        compiler_params=_cparams(1),
        name="norm_modulate",
    )(*xs, g_row, mod4, mod4)


def _final_norm_kernel(x_ref, g_ref, o_ref):
    x = x_ref[...]
    o_ref[...] = x * lax.rsqrt(jnp.mean(x * x, axis=-1, keepdims=True) + EPS) * g_ref[...]


def _final_norm(x, g_row, row0, n_rows):
    return pl.pallas_call(
        _final_norm_kernel,
        out_shape=jax.ShapeDtypeStruct((n_rows, D_MODEL), F32),
        grid=(n_rows // NM_ROWS,),
        in_specs=[
            pl.BlockSpec((NM_ROWS, D_MODEL), lambda i: (row0 // NM_ROWS + i, 0)),
            pl.BlockSpec((1, D_MODEL), lambda i: (0, 0)),
        ],
        out_specs=pl.BlockSpec((NM_ROWS, D_MODEL), lambda i: (i, 0)),
        compiler_params=_cparams(1),
        name="final_norm",
    )(x, g_row)


def _cast_weight(w_ref, wb_ref):
    @pl.when(pl.program_id(1) == 0)
    def _():
        wb_ref[...] = w_ref[...].astype(BF16)


def _mm_kernel(a_ref, w_ref, o_ref, wb_ref, *, relu2, w_is_nk):
    _cast_weight(w_ref, wb_ref)
    if w_is_nk:
        acc = lax.dot_general(a_ref[...], wb_ref[...], (((1,), (1,)), ((), ())), preferred_element_type=F32)
    else:
        acc = jnp.dot(a_ref[...], wb_ref[...], preferred_element_type=F32)
    if relu2:
        acc = jnp.square(jnp.maximum(acc, 0.0))
    o_ref[...] = acc.astype(o_ref.dtype)


def _matmul(a, w, layer, n_cols, *, relu2, out_dtype, w_is_nk=False):
    k = a.shape[1]
    if w_is_nk:
        w_spec, w_block = pl.BlockSpec((None, TN, k), lambda j, i: (layer, j, 0)), (TN, k)
    else:
        w_spec, w_block = pl.BlockSpec((None, k, TN), lambda j, i: (layer, 0, j)), (k, TN)
    return pl.pallas_call(
        functools.partial(_mm_kernel, relu2=relu2, w_is_nk=w_is_nk),
        out_shape=jax.ShapeDtypeStruct((M_ALL, n_cols), out_dtype),
        grid=(n_cols // TN, M_ALL // TM),
        in_specs=[pl.BlockSpec((TM, k), lambda j, i: (i, 0)), w_spec],
        out_specs=pl.BlockSpec((TM, TN), lambda j, i: (i, j)),
        scratch_shapes=[pltpu.VMEM(w_block, BF16)],
        compiler_params=_cparams(2),
        name="proj_relu2" if relu2 else "proj",
    )(a, w)


def _mm_res_kernel(*refs):
    a_ref, w_ref, *x_refs, g_ref, o_ref, wb_ref = refs
    _cast_weight(w_ref, wb_ref)
    acc = jnp.dot(a_ref[...], wb_ref[...], preferred_element_type=F32)
    tile = pl.program_id(1)
    x = _x_tile(tile >= M_PROMPT // TM, x_refs)
    shape3 = (TM // GROUP, GROUP, TN)
    o_ref[...] = (x.reshape(shape3) + _tile_mod(tile, TM, g_ref) * acc.reshape(shape3)).reshape(TM, TN)


def _matmul_residual(a, w, w_layer, x, mod4, layer, gate_col, *, kblk=0, kc=D_MODEL):
    nj = D_MODEL // TN
    xs = x if isinstance(x, tuple) else (x,)
    return pl.pallas_call(
        _mm_res_kernel,
        out_shape=jax.ShapeDtypeStruct((M_ALL, D_MODEL), F32),
        grid=(nj, M_ALL // TM),
        in_specs=[
            pl.BlockSpec((TM, kc), lambda j, i: (i, kblk)),
            pl.BlockSpec((None, kc, TN), lambda j, i: (w_layer, kblk, j)),
        ] + _x_specs(x, TM, TN, lambda j, i: i, lambda j, i: j) + [
            _mod_spec(TN, layer, lambda j, i: gate_col * nj + j),
        ],
        out_specs=pl.BlockSpec((TM, TN), lambda j, i: (i, j)),
        scratch_shapes=[pltpu.VMEM((kc, TN), BF16)],
        input_output_aliases={} if isinstance(x, tuple) else {2: 0},
        compiler_params=_cparams(2),
        name="proj_residual",
    )(a, w, *xs, mod4)


CS_ROWS = 256


def _bg_kernel(a_ref, w_ref, alog_ref, dtb_ref, o_ref):
    row_w = lax.broadcasted_iota(jnp.int32, w_ref.shape, 0)
    w = jnp.where(row_w < 2 * B_HEADS, w_ref[...], 0.0).astype(BF16)
    acc = lax.dot_general(a_ref[...], w, (((1,), (1,)), ((), ())), preferred_element_type=F32)
    lane = lax.broadcasted_iota(jnp.int32, (CS_ROWS, LANES), 1)
    z = acc + dtb_ref[...]
    softplus = jnp.maximum(z, 0.0) + jnp.log1p(jnp.exp(-jnp.abs(z)))
    g = -jnp.exp(alog_ref[...]) * softplus
    is_sample = pl.program_id(0) * TM >= M_PROMPT
    ii = lax.broadcasted_iota(jnp.int32, (CS_ROWS, CS_ROWS), 0)
    jj = lax.broadcasted_iota(jnp.int32, (CS_ROWS, CS_ROWS), 1)
    chunk_i = jnp.where(is_sample, lax.shift_right_logical(ii, 5), lax.shift_right_logical(ii, 6))
    chunk_j = jnp.where(is_sample, lax.shift_right_logical(jj, 5), lax.shift_right_logical(jj, 6))
    ltri = jnp.where(chunk_i == chunk_j, jnp.where(ii >= jj, 1.0, 0.0), 0.0).astype(BF16)
    for r in range(TM // CS_ROWS):
        rows = slice(r * CS_ROWS, (r + 1) * CS_ROWS)
        gc = pltpu.roll(_dot_exact_lhs(ltri, g[rows]), B_HEADS, 1)
        o_ref[rows, :] = jnp.where(lane < B_HEADS, _sigmoid(acc[rows]),
                                   jnp.where(lane < 2 * B_HEADS, g[rows], gc))


def _beta_gate(a, w_in_t, layer_j, alog_row, dtb_row):
    return pl.pallas_call(
        _bg_kernel,
        out_shape=jax.ShapeDtypeStruct((M_ALL, LANES), F32),
        grid=(M_ALL // TM,),
        in_specs=[
            pl.BlockSpec((TM, D_MODEL), lambda i: (i, 0)),
            pl.BlockSpec((None, LANES, D_MODEL), lambda i: (layer_j, EVEN_MAIN // LANES, 0)),
            pl.BlockSpec((1, LANES), lambda i: (0, 0)),
            pl.BlockSpec((1, LANES), lambda i: (0, 0)),
        ],
        out_specs=pl.BlockSpec((TM, LANES), lambda i: (i, 0)),
        compiler_params=_cparams(1),
        name="beta_gate",
    )(a, w_in_t, alog_row, dtb_row)


class _Seg:
    def __init__(self, batch, length, row0, step, chunk, pos0, carry=True):
        self.batch, self.length, self.row0, self.step, self.chunk, self.pos0 = batch, length, row0, step, chunk, pos0
        self.nblk = length // step
        self.nch = step // chunk
        self.carry = carry

    def rb(self, b, n):
        return (self.row0 + b * self.length) // self.step + n


SEG_PROMPT = _Seg(BATCH, SEQ, 0, 256, CHUNK, 0)
SAMPLE_SEQS = 8
SEG_SAMPLE_GROUPS = _Seg(DEC_BATCH // SAMPLE_SEQS, SAMPLE_SEQS * DEC_SEQ, M_PROMPT, SAMPLE_SEQS * DEC_SEQ, DEC_SEQ,
                         PAST_LEN, carry=False)


def _chunk_ids(t, chunk, width):
    return lax.shift_right_logical(lax.broadcasted_iota(jnp.int32, (t, width), 0), chunk.bit_length() - 1)


HPA = 4


def _rope(x, cos, sin):
    return x * cos + pltpu.roll(x, A_DK // 2, 1) * sin


def _ret_kernel(*refs, chunk, nch, carry, aliased):
    if aliased:
        refs = refs[:-4] + refs[-3:]
    (q_ref, k_ref, v_ref, g_ref, cos_ref, sin_ref, dm_ref, qd_ref, kd_ref, cd_ref, gn_ref, s0_ref,
     y_ref, st_ref, s_scr) = refs
    n = pl.program_id(2)
    t = chunk * nch
    hs = range(HPA)
    kcols = [slice(hh * A_DK, (hh + 1) * A_DK) for hh in hs]
    vcols = [slice(hh * A_DV, (hh + 1) * A_DV) for hh in hs]

    if carry:
        @pl.when(n == 0)
        def _():
            s_scr[...] = s0_ref[...]

    cos = cos_ref[...]
    sin = sin_ref[...]
    cid_k = _chunk_ids(t, chunk, A_DK)
    cid_v = _chunk_ids(t, chunk, A_DV)

    def per_chunk(x, cid):
        if nch == 1:
            return x
        return jnp.concatenate([jnp.where(cid == c, x, 0.0) for c in range(nch)], axis=1)

    q = [_rope(q_ref[:, kcols[hh]], cos, sin) for hh in hs]
    k = [_rope(k_ref[:, kcols[hh]], cos, sin) * (A_DK ** -0.5) for hh in hs]
    v = [v_ref[:, vcols[hh]] for hh in hs]
    scores = [_dot_nt(q[hh], k[hh]) * dm_ref[hh] for hh in hs]
    kv = [_dot_tn(k[hh] * kd_ref[hh], per_chunk(v[hh], cid_v)) for hh in hs]
    states = []
    for hh in hs:
        cd = cd_ref[hh]
        kv_c = [kv[hh][:, c * A_DV:(c + 1) * A_DV] for c in range(nch)]
        if carry:
            chain = [s_scr[hh]]
            for c in range(nch):
                chain.append(chain[-1] * cd + kv_c[c])
            s_scr[hh] = chain[nch]
            states.append(chain[:nch])
        else:
            states.append([s0_ref[c, hh] for c in range(nch)])
            for c in range(nch):
                st_ref[c, hh] = states[hh][c] * cd + kv_c[c]
    o = [_dot(jnp.concatenate([per_chunk(q[hh] * qd_ref[hh], cid_k), scores[hh]], axis=1),
              jnp.concatenate(states[hh] + [v[hh]], axis=0)) for hh in hs]
    for hh in hs:
        mu = jnp.mean(o[hh], axis=-1, keepdims=True)
        d = o[hh] - mu
        var = jnp.mean(d * d, axis=-1, keepdims=True)
        on = d * lax.rsqrt(var + EPS) * gn_ref[hh]
        y_ref[:, vcols[hh]] = (_silu(g_ref[:, vcols[hh]]) * on).astype(y_ref.dtype)

    if carry:
        @pl.when(n == pl.num_programs(2) - 1)
        def _():
            st_ref[...] = s_scr[...]


def _retention_tables(chunk, nch):
    t = chunk * nch
    lg = np.log1p(-np.exp2(-5.0 - np.arange(A_HEADS, dtype=np.float64)))
    idx = np.arange(t)
    pos = idx % chunk
    rel = pos[:, None] - pos[None, :]
    same = (idx[:, None] // chunk) == (idx[None, :] // chunk)
    dm = np.where((same & (rel >= 0))[None], np.exp(np.maximum(rel, 0)[None] * lg[:, None, None]), 0.0)
    qd = np.exp((pos[None, :] + 1.0) * lg[:, None])
    kd = np.exp((chunk - 1.0 - pos)[None, :] * lg[:, None])
    cd = np.exp(chunk * lg)
    bc = lambda x, w: jnp.asarray(np.broadcast_to(x[:, :, None], x.shape + (w,)), F32)
    return jnp.asarray(dm, F32), bc(qd, A_DK), bc(kd, A_DK), bc(cd[:, None], A_DV)


def _rope_tables(pos0, length, period):
    half = A_DK // 2
    inv_freq = ROPE_BASE ** (-np.arange(half, dtype=np.float64) / half)
    ang = (pos0 + np.arange(length, dtype=np.float64) % period)[:, None] * inv_freq[None, :]
    cos, sin = np.cos(ang), np.sin(ang)
    return (jnp.asarray(np.concatenate([cos, cos], axis=-1), F32),
            jnp.asarray(np.concatenate([-sin, sin], axis=-1), F32))


def _retention(seg, p, y_in, layer_j, gn4, s0):
    t, c = seg.step, seg.chunk
    dm, qd, kd, cd = _retention_tables(c, seg.nch)
    cos, sin = _rope_tables(seg.pos0, seg.length, seg.length if seg.carry else c)
    rb = seg.rb
    kw, vw = HPA * A_DK, HPA * A_DV
    hgroups = A_HEADS // HPA
    k_blk = A_QK // kw
    v_blk = 2 * A_QK // vw
    g_blk = (2 * A_QK + A_V) // vw
    state_spec = pl.BlockSpec((None if seg.carry else seg.nch, HPA, A_DK, A_DV), lambda b, h, n: (b, h, 0, 0))
    in_specs = [
        pl.BlockSpec((t, kw), lambda b, h, n: (rb(b, n), h)),
        pl.BlockSpec((t, kw), lambda b, h, n: (rb(b, n), k_blk + h)),
        pl.BlockSpec((t, vw), lambda b, h, n: (rb(b, n), v_blk + h)),
        pl.BlockSpec((t, vw), lambda b, h, n: (rb(b, n), g_blk + h)),
        pl.BlockSpec((t, A_DK), lambda b, h, n: (n, 0)),
        pl.BlockSpec((t, A_DK), lambda b, h, n: (n, 0)),
        pl.BlockSpec((HPA, t, t), lambda b, h, n: (h, 0, 0)),
        pl.BlockSpec((HPA, t, A_DK), lambda b, h, n: (h, 0, 0)),
        pl.BlockSpec((HPA, t, A_DK), lambda b, h, n: (h, 0, 0)),
        pl.BlockSpec((HPA, 1, A_DV), lambda b, h, n: (h, 0, 0)),
        pl.BlockSpec((None, HPA, 1, A_DV), lambda b, h, n: (layer_j, h, 0, 0)),
        state_spec,
    ]
    args = [p, p, p, p, cos, sin, dm, qd, kd, cd, gn4, s0]
    aliases = {}
    if y_in is not None:
        in_specs.append(pl.BlockSpec(memory_space=pl.ANY))
        args.append(y_in)
        aliases = {len(args) - 1: 0}
    return pl.pallas_call(
        functools.partial(_ret_kernel, chunk=c, nch=seg.nch, carry=seg.carry, aliased=y_in is not None),
        out_shape=(jax.ShapeDtypeStruct((M_ALL, D_MODEL), BF16), jax.ShapeDtypeStruct(s0.shape, F32)),
        grid=(seg.batch, hgroups, seg.nblk),
        in_specs=in_specs,
        out_specs=(pl.BlockSpec((t, vw), lambda b, h, n: (rb(b, n), h)), state_spec),
        scratch_shapes=[pltpu.VMEM((HPA, A_DK, A_DV), F32)],
        input_output_aliases=aliases,
        compiler_params=_cparams(3),
        name="retention",
    )(*args)


HPS = 8


def _unit_lower_solve(a, rhs, same_blk, nblocks):
    hs = range(len(a))
    n = a[0].shape[0]
    d = [jnp.where(same_blk, a[h], 0.0) for h in hs]
    y = [jnp.concatenate([a[h] - d[h], rhs[h]], axis=1) for h in hs]
    sign = -1.0
    for _ in range(BD.bit_length() - 2):
        r = [_dot(d[h], jnp.concatenate([d[h], y[h]], axis=1)) for h in hs]
        d = [r[h][:, :n] for h in hs]
        y = [y[h] + sign * r[h][:, n:] for h in hs]
        sign = 1.0
    y = [y[h] + _dot(d[h], y[h]) for h in hs]
    f = [y[h][:, :n] for h in hs]
    y = [y[h][:, n:] for h in hs]
    if nblocks == 1:
        return y
    if nblocks == 2:
        return [y[h] - _dot(f[h], y[h]) for h in hs]
    r = [_dot(f[h], jnp.concatenate([f[h], y[h]], axis=1)) for h in hs]
    f2 = [r[h][:, :n] for h in hs]
    y = [y[h] - r[h][:, n:] for h in hs]
    return [y[h] + _dot(f2[h], y[h]) for h in hs]


def _delta_kernel(*refs, chunk, nch, carry):
    (q_ref, k_ref, v_ref, z_ref, bg_ref, cwq_ref, cwk_ref, cwv_ref, tq_ref, tk_ref, tv_ref, dn_ref, s0_ref,
     _, y_ref, st_ref, s_scr, tail_scr) = refs
    hg = pl.program_id(1)
    n = pl.program_id(2)
    t = chunk * nch
    log2_chunk = chunk.bit_length() - 1
    tail_refs = (tq_ref, tk_ref, tv_ref)
    hs = range(HPS)
    cols = [slice(hh * B_DK, (hh + 1) * B_DK) for hh in hs]

    if carry:
        @pl.when(n == 0)
        def _():
            s_scr[...] = s0_ref[...]
            for part in range(3):
                tail_scr[part] = tail_refs[part][...]

    ii = lax.broadcasted_iota(jnp.int32, (t, t), 0)
    jj = lax.broadcasted_iota(jnp.int32, (t, t), 1)
    same_chunk = lax.shift_right_logical(ii, log2_chunk) == lax.shift_right_logical(jj, log2_chunk)
    lower = jnp.logical_and(same_chunk, ii >= jj)
    strict = jnp.logical_and(same_chunk, ii > jj)
    same_blk = lax.shift_right_logical(ii, BD.bit_length() - 1) == lax.shift_right_logical(jj, BD.bit_length() - 1)
    chunk_id = _chunk_ids(t, chunk, LANES)
    bg = bg_ref[...]
    lane = lax.broadcasted_iota(jnp.int32, bg.shape, 1)
    dn = dn_ref[...]

    def conv(u_ref, cw_ref, part, cs):
        u = u_ref[:, cs]
        w = cw_ref[:, cs]
        out = u * w[CONV_W - 1:CONV_W, :]
        first = 8 - (CONV_W - 1)
        if carry:
            ext = jnp.concatenate([tail_scr[part, :, cs], u], axis=0)
            for i in range(CONV_W - 1):
                out = out + ext[first + i:first + i + t, :] * w[i:i + 1, :]
            tail_scr[part, :, cs] = u[t - 8:t, :]
        else:
            ext = jnp.concatenate([tail_refs[part][:, :, cs], u.reshape(nch, chunk, B_DK)], axis=1)
            for i in range(CONV_W - 1):
                out = out + (ext[:, first + i:first + i + chunk, :] * w[i:i + 1, :][None]).reshape(t, B_DK)
        return _silu(out)

    cq = [conv(q_ref, cwq_ref, 0, cs) for cs in cols]
    ck = [conv(k_ref, cwk_ref, 1, cs) for cs in cols]
    v = [conv(v_ref, cwv_ref, 2, cs) for cs in cols]
    q = [x * lax.rsqrt(jnp.sum(x * x, axis=-1, keepdims=True) + EPS) * (B_DK ** -0.5) for x in cq]
    k = [x * lax.rsqrt(jnp.sum(x * x, axis=-1, keepdims=True) + EPS) for x in ck]

    def head_column(lane0, hh):
        return jnp.sum(jnp.where(lane == lane0 + hg * HPS + hh, bg, 0.0), axis=-1, keepdims=True)

    beta = [head_column(0, hh) for hh in hs]
    gc = [head_column(2 * B_HEADS, hh) for hh in hs]
    gcb = [jnp.broadcast_to(x, (t, LANES)) for x in gc]

    def decay_mask(hh):
        gct = gcb[hh].T
        row = gct[:t] if t <= LANES else jnp.concatenate([gct] * (t // LANES), axis=0)
        col = jnp.broadcast_to(gc[hh], (t, t))
        return jnp.where(lower, jnp.exp(col - row), 0.0)

    dmask = [decay_mask(hh) for hh in hs]
    kb = [k[hh] * beta[hh] for hh in hs]
    kq = [_dot_nt(jnp.concatenate([kb[hh], q[hh]], axis=0), k[hh]) for hh in hs]
    a = [jnp.where(strict, kq[hh][:t] * dmask[hh], 0.0) for hh in hs]
    qk = [kq[hh][t:] * dmask[hh] for hh in hs]
    eg = [jnp.exp(x) for x in gcb]
    uw = _unit_lower_solve(a, [jnp.concatenate([v[hh] * beta[hh], kb[hh] * eg[hh]], axis=1) for hh in hs],
                           same_blk, chunk // BD)
    u = [x[:, :B_DV] for x in uw]
    w = [x[:, B_DV:] for x in uw]

    qw = [_dot(qk[hh], jnp.concatenate([w[hh], u[hh]], axis=1)) for hh in hs]
    qp = [q[hh] * eg[hh] - qw[hh][:, :B_DV] for hh in hs]
    op = [x[:, B_DV:] for x in qw]
    last = [[gcb[hh][(c + 1) * chunk - 1:(c + 1) * chunk, :] for c in range(nch)] for hh in hs]
    lastb = [jnp.concatenate([jnp.broadcast_to(l, (chunk, LANES)) for l in last[hh]], axis=0) for hh in hs]
    kdec = [k[hh] * jnp.exp(lastb[hh] - gcb[hh]) for hh in hs]
    wu = [jnp.concatenate([jnp.where(chunk_id == c, w[hh], 0.0) for c in range(nch)]
                          + [jnp.where(chunk_id == c, u[hh], 0.0) for c in range(nch)], axis=1) for hh in hs]
    xn = [_dot_tn(kdec[hh], wu[hh]) for hh in hs]

    s = [s_scr[hh] if carry else None for hh in hs]
    outs = [[] for _ in hs]
    for c in range(nch):
        rows = slice(c * chunk, (c + 1) * chunk)
        if not carry:
            s = [s0_ref[c, hh] for hh in hs]
        r = [_dot(jnp.concatenate([xn[hh][:, c * B_DV:(c + 1) * B_DV], qp[hh][rows]], axis=0), s[hh]) for hh in hs]
        for hh in hs:
            outs[hh].append(r[hh][B_DK:] + op[hh][rows])
        s = [s[hh] * jnp.exp(last[hh][c]) - r[hh][:B_DK] + xn[hh][:, (nch + c) * B_DV:(nch + c + 1) * B_DV]
             for hh in hs]
        if not carry:
            for hh in hs:
                st_ref[c, hh] = s[hh]
    for hh in hs:
        if carry:
            s_scr[hh] = s[hh]
        o = jnp.concatenate(outs[hh], axis=0) if nch > 1 else outs[hh][0]
        on = o * lax.rsqrt(jnp.mean(o * o, axis=-1, keepdims=True) + EPS) * dn
        y_ref[:, cols[hh]] = (on * _silu(z_ref[:, cols[hh]])).astype(y_ref.dtype)

    if carry:
        @pl.when(n == pl.num_programs(2) - 1)
        def _():
            st_ref[...] = s_scr[...]


def _gated_delta(seg, p, bg, y_in, layer_j, conv_w, tails, dn3, s0):
    t, c = seg.step, seg.chunk
    rb = seg.rb
    wide = HPS * B_DK
    q_blk = (2 * A_QK + 2 * A_V) // wide
    z_blk = (2 * A_QK + 2 * A_V + CONV_CH) // wide
    y_blk = A_V // wide
    hgroups = B_HEADS // HPS
    seqs = None if seg.carry else seg.nch

    def pcol(off):
        return pl.BlockSpec((t, wide), lambda b, h, n: (rb(b, n), off + h))

    def wcol(part):
        return pl.BlockSpec((None, CONV_W, wide), lambda b, h, n: (layer_j, 0, part * hgroups + h))

    def tcol(part):
        return pl.BlockSpec((seqs, 8, wide), lambda b, h, n: (b, 0, part * hgroups + h))

    state_spec = pl.BlockSpec((seqs, HPS, B_DK, B_DV), lambda b, h, n: (b, h, 0, 0))
    in_specs = [
        pcol(q_blk), pcol(q_blk + hgroups), pcol(q_blk + 2 * hgroups), pcol(z_blk),
        pl.BlockSpec((t, LANES), lambda b, h, n: (rb(b, n), 0)),
        wcol(0), wcol(1), wcol(2), tcol(0), tcol(1), tcol(2),
        pl.BlockSpec((None, 1, B_DV), lambda b, h, n: (layer_j, 0, 0)),
        state_spec,
        pl.BlockSpec(memory_space=pl.ANY),
    ]
    return pl.pallas_call(
        functools.partial(_delta_kernel, chunk=c, nch=seg.nch, carry=seg.carry),
        out_shape=(jax.ShapeDtypeStruct((M_ALL, D_MODEL), BF16), jax.ShapeDtypeStruct(s0.shape, F32)),
        grid=(seg.batch, hgroups, seg.nblk),
        in_specs=in_specs,
        out_specs=(pl.BlockSpec((t, wide), lambda b, h, n: (rb(b, n), y_blk + h)), state_spec),
        scratch_shapes=[pltpu.VMEM((HPS, B_DK, B_DV), F32), pltpu.VMEM((3, 8, wide), F32)],
        input_output_aliases={13: 0},
        compiler_params=_cparams(3),
        name="gated_delta",
    )(p, p, p, p, bg, conv_w, conv_w, conv_w, tails, tails, tails, dn3, s0, y_in)


TQ = 256
BAND = BAND_PREV * CHUNK
WIN = BAND + TQ
NEG = -1e30
ATTN_UNROLL = 14


def _softmax_pv(s, v_ones):
    p = jnp.exp(s - jnp.max(s, axis=-1, keepdims=True))
    o = _dot(p, v_ones)
    return o[:, :C_HD] / o[:, C_HD:]


BIAS_LANES = 1024


def _bias_vectors(rel_bias, cols, base):
    s = np.arange(BIAS_LANES)
    d = np.where(s < cols, s, s - BIAS_LANES)
    return rel_bias[np.clip(base - d, -REL_CLIP, REL_CLIP) + REL_CLIP].T[:, None, :]


def _toeplitz(vec, rows, cols):
    x = jnp.broadcast_to(vec, (rows, BIAS_LANES))
    return pltpu.roll(x, 0, 1, stride=1, stride_axis=0)[:, :cols]


def _attn_prompt_kernel(q_ref, k_ref, v_ref, vec_ref, o_ref, kp_scr, vp_scr, bias_scr):
    kp_scr[0:BAND, :] = jnp.zeros((BAND, C_HD), BF16)
    vp_scr[0:BAND, 0:C_HD] = jnp.zeros((BAND, C_HD), BF16)
    kp_scr[BAND:, :] = k_ref[...].astype(BF16)
    vp_scr[BAND:, 0:C_HD] = v_ref[...].astype(BF16)
    vp_scr[:, C_HD:] = jnp.ones((BAND + SEQ, C_HD), BF16)
    r = lax.broadcasted_iota(jnp.int32, (TQ, WIN), 0)
    w = lax.broadcasted_iota(jnp.int32, (TQ, WIN), 1)
    j = w - lax.shift_left(lax.shift_right_logical(r, CHUNK.bit_length() - 1), CHUNK.bit_length() - 1)
    inside = jnp.logical_and(j >= 0, j < BAND + CHUNK)
    bias_scr[...] = jnp.where(inside, _toeplitz(vec_ref[...], TQ, WIN), NEG)

    def block(n, masked):
        start = n * TQ if masked else pl.multiple_of(n * TQ, TQ)
        rows = pl.ds(start, TQ)
        s = _dot_nt(q_ref[rows, :] * (C_HD ** -0.5), kp_scr[pl.ds(start, WIN), :]) + bias_scr[...]
        if masked:
            col = lax.broadcasted_iota(jnp.int32, s.shape, 1)
            s = jnp.where(col + start >= BAND, s, NEG)
        o_ref[rows, :] = _softmax_pv(s, vp_scr[pl.ds(start, WIN), :]).astype(o_ref.dtype)

    n_masked = BAND // TQ
    for n in range(n_masked):
        block(n, True)

    def body(n, carry):
        block(n, False)
        return carry

    lax.fori_loop(n_masked, SEQ // TQ, body, 0, unroll=ATTN_UNROLL)


def _attention_prompt(p, rel_bias):
    k_blk = C_WIDTH // C_HD
    return pl.pallas_call(
        _attn_prompt_kernel,
        out_shape=jax.ShapeDtypeStruct((M_ALL, C_WIDTH), BF16),
        grid=(BATCH, C_HEADS),
        in_specs=[
            pl.BlockSpec((SEQ, C_HD), lambda b, h: (b, h)),
            pl.BlockSpec((SEQ, C_HD), lambda b, h: (b, k_blk + h)),
            pl.BlockSpec((SEQ, C_HD), lambda b, h: (b, 2 * k_blk + h)),
            pl.BlockSpec((None, 1, BIAS_LANES), lambda b, h: (h, 0, 0)),
        ],
        out_specs=pl.BlockSpec((SEQ, C_HD), lambda b, h: (b, h)),
        scratch_shapes=[pltpu.VMEM((BAND + SEQ, C_HD), BF16), pltpu.VMEM((BAND + SEQ, 2 * C_HD), BF16),
                        pltpu.VMEM((TQ, WIN), F32)],
        compiler_params=_cparams(2),
        name="band_attention_prompt",
    )(p, p, p, _bias_vectors(rel_bias, WIN, BAND))


HB = 8


def _band_state_kernel(k_ref, v_ref, ko_ref, vo_ref):
    for hh in range(HB):
        cs = slice(hh * C_HD, (hh + 1) * C_HD)
        ko_ref[:, hh, :] = k_ref[:, cs]
        vo_ref[:, hh, :] = v_ref[:, cs]


def _band_state(p):
    wide = HB * C_HD
    k_blk = C_WIDTH // wide
    last = SEQ // BAND - 1
    state = jax.ShapeDtypeStruct((BATCH, BAND, C_HEADS, C_HD), F32)
    state_spec = pl.BlockSpec((None, BAND, HB, C_HD), lambda b, g: (b, 0, g, 0))
    return pl.pallas_call(
        _band_state_kernel,
        out_shape=(state, state),
        grid=(BATCH, C_HEADS // HB),
        in_specs=[
            pl.BlockSpec((BAND, wide), lambda b, g: (b * (SEQ // BAND) + last, k_blk + g)),
            pl.BlockSpec((BAND, wide), lambda b, g: (b * (SEQ // BAND) + last, 2 * k_blk + g)),
        ],
        out_specs=(state_spec, state_spec),
        compiler_params=_cparams(2),
        name="band_state",
    )(p, p)


def _attn_sample_kernel(q_ref, kn_ref, vn_ref, ck_ref, cv_ref, vec_ref, _, o_ref, ko_ref, vo_ref):
    lc = ck_ref.shape[0]
    hs = range(HB)
    cols = [slice(hh * C_HD, (hh + 1) * C_HD) for hh in hs]
    kn = [kn_ref[:, cs] for cs in cols]
    vn = [vn_ref[:, cs] for cs in cols]
    for hh in hs:
        ko_ref[:, hh, :] = kn[hh]
        vo_ref[:, hh, :] = vn[hh]
    kk = [jnp.concatenate([ck_ref[:, hh, :], kn[hh]], axis=0) for hh in hs]
    vv = [jnp.concatenate([cv_ref[:, hh, :], vn[hh]], axis=0) for hh in hs]
    s = [_dot_nt(q_ref[:, cols[hh]] * (C_HD ** -0.5), kk[hh]) + _toeplitz(vec_ref[hh], DEC_SEQ, lc + DEC_SEQ)
         for hh in hs]
    m = [jnp.max(x, axis=-1, keepdims=True) for x in s]
    e = [jnp.exp(s[hh] - m[hh]) for hh in hs]
    l = [jnp.sum(x, axis=-1, keepdims=True) for x in e]
    o = [_dot(e[hh], vv[hh]) for hh in hs]
    for hh in hs:
        o_ref[:, cols[hh]] = (o[hh] / l[hh]).astype(o_ref.dtype)


def _attention_sample(p, o_in, cache_k, cache_v, rel_bias):
    lc = cache_k.shape[1]
    rb0 = M_PROMPT // DEC_SEQ
    wide = HB * C_HD
    k_blk = C_WIDTH // wide
    state = jax.ShapeDtypeStruct((DEC_BATCH, DEC_SEQ, C_HEADS, C_HD), F32)
    state_spec = pl.BlockSpec((None, DEC_SEQ, HB, C_HD), lambda b, g: (b, 0, g, 0))
    cache_spec = pl.BlockSpec((None, lc, HB, C_HD), lambda b, g: (b, 0, g, 0))
    return pl.pallas_call(
        _attn_sample_kernel,
        out_shape=(jax.ShapeDtypeStruct((M_ALL, C_WIDTH), BF16), state, state),
        grid=(DEC_BATCH, C_HEADS // HB),
        in_specs=[
            pl.BlockSpec((DEC_SEQ, wide), lambda b, g: (rb0 + b, g)),
            pl.BlockSpec((DEC_SEQ, wide), lambda b, g: (rb0 + b, k_blk + g)),
            pl.BlockSpec((DEC_SEQ, wide), lambda b, g: (rb0 + b, 2 * k_blk + g)),
            cache_spec, cache_spec,
            pl.BlockSpec((HB, 1, BIAS_LANES), lambda b, g: (g, 0, 0)),
            pl.BlockSpec(memory_space=pl.ANY),
        ],
        out_specs=(pl.BlockSpec((DEC_SEQ, wide), lambda b, g: (rb0 + b, g)), state_spec, state_spec),
        input_output_aliases={6: 0},
        compiler_params=_cparams(2),
        name="band_attention_sample",
    )(p, p, p, cache_k, cache_v, _bias_vectors(rel_bias, lc + DEC_SEQ, lc), o_in)


def _conv_tail(state_conv):
    return jnp.pad(state_conv, ((0, 0), (8 - (CONV_W - 1), 0), (0, 0)))


def _lane_row(x, lanes):
    row = jnp.zeros((LANES,), F32)
    for l0 in lanes:
        row = row.at[l0:l0 + B_HEADS].set(x)
    return row[None, :]


def kernel(x_prompt, x_sample, c_prompt, c_sample, state_ret_0, state_delta_0, state_conv_0, cache_k_1, cache_v_1, state_ret_2, state_delta_2, state_conv_2, cache_k_3, cache_v_3, norm_mix_g, norm_ffn_g, w_mod, b_mod, w_in_even, w_out_even, a_norm_g, b_conv_w, b_a_log, b_dt_bias, b_norm_g, w_in_odd, w_out_odd, c_rel_bias, w_ff1, w_ff2, final_norm_g):
    x = (x_prompt.reshape(M_PROMPT, D_MODEL), x_sample.reshape(M_SAMPLE, D_MODEL))
    c_all = jnp.concatenate([c_sample, c_prompt, jnp.zeros((N_SEQ_PAD - N_SEQ, D_MODEL), F32)], axis=0)
    mod4 = _modulation(c_all, w_mod, b_mod).reshape(DEPTH, N_SEQ_PAD, 1, 6 * D_MODEL)

    sample_states = {0: (state_ret_0, state_delta_0, state_conv_0), 1: (cache_k_1, cache_v_1),
                     2: (state_ret_2, state_delta_2, state_conv_2), 3: (cache_k_3, cache_v_3)}
    w_in_even_t = jnp.swapaxes(w_in_even, 1, 2)
    gn4 = a_norm_g.reshape(-1, A_HEADS, 1, A_DV)
    dn3 = b_norm_g.reshape(-1, 1, B_DV)
    zeros_ret = jnp.zeros((BATCH, A_HEADS, A_DK, A_DV), F32)
    zeros_delta = jnp.zeros((BATCH, B_HEADS, B_DK, B_DV), F32)
    zeros_tail = jnp.zeros((BATCH, 8, CONV_CH), F32)

    p_states, s_states = [], []
    for l in range(DEPTH):
        j = l // 2
        h = _norm_modulate(x, norm_mix_g[l][None, :], mod4, l, 0, 1)
        if l % 2 == 0:
            p = _matmul(h, w_in_even_t, j, EVEN_MAIN, relu2=False, out_dtype=F32, w_is_nk=True)
            bg = _beta_gate(h, w_in_even_t, j, _lane_row(b_a_log[j], (B_HEADS,)), _lane_row(b_dt_bias[j], (B_HEADS,)))
            s_ret, s_delta, s_conv = sample_states[l]
            y, p_ret = _retention(SEG_PROMPT, p, None, j, gn4, zeros_ret)
            y, s_ret_new = _retention(SEG_SAMPLE_GROUPS, p, y, j, gn4, s_ret)
            y, p_delta = _gated_delta(SEG_PROMPT, p, bg, y, j, b_conv_w, zeros_tail, dn3, zeros_delta)
            y, s_delta_new = _gated_delta(SEG_SAMPLE_GROUPS, p, bg, y, j, b_conv_w, _conv_tail(s_conv), dn3, s_delta)
            c0 = 2 * A_QK + 2 * A_V
            p_conv = jnp.stack([lax.slice(p, ((b + 1) * SEQ - (CONV_W - 1), c0), ((b + 1) * SEQ, c0 + CONV_CH))
                                for b in range(BATCH)])
            s_conv_new = lax.slice(p, (M_PROMPT, c0), (M_ALL, c0 + CONV_CH)).reshape(
                DEC_BATCH, DEC_SEQ, CONV_CH)[:, DEC_SEQ - (CONV_W - 1):]
            p_states.append((p_ret, p_delta, p_conv))
            s_states.append((s_ret_new, s_delta_new, s_conv_new))
            x = _matmul_residual(y, w_out_even, j, x, mod4, l, 2)
        else:
            p = _matmul(h, w_in_odd, j, 3 * C_WIDTH, relu2=False, out_dtype=F32)
            cache_k, cache_v = sample_states[l]
            o = _attention_prompt(p, c_rel_bias[j])
            o, sk, sv = _attention_sample(p, o, cache_k, cache_v, c_rel_bias[j])
            p_states.append(_band_state(p))
            s_states.append((sk, sv))
            x = _matmul_residual(o, w_out_odd, j, x, mod4, l, 2)
        h = _norm_modulate(x, norm_ffn_g[l][None, :], mod4, l, 3, 4)
        f = _matmul(h, w_ff1, l, D_FF, relu2=True, out_dtype=BF16)
        for kblk in range(D_FF // D_MODEL):
            x = _matmul_residual(f, w_ff2, l, x, mod4, l, 5, kblk=kblk)

    g_row = final_norm_g[None, :]
    y_prompt = _final_norm(x, g_row, 0, M_PROMPT).reshape(BATCH, SEQ, D_MODEL)
    y_sample = _final_norm(x, g_row, M_PROMPT, M_SAMPLE).reshape(DEC_BATCH, DEC_SEQ, D_MODEL)
    (p_ret_0, p_delta_0, p_conv_0), (p_k_1, p_v_1), (p_ret_2, p_delta_2, p_conv_2), (p_k_3, p_v_3) = p_states
    (s_ret_0, s_delta_0, s_conv_0), (s_k_1, s_v_1), (s_ret_2, s_delta_2, s_conv_2), (s_k_3, s_v_3) = s_states
    return (y_prompt, y_sample,
            p_ret_0, s_ret_0, p_delta_0, s_delta_0, p_conv_0, s_conv_0,
            p_k_1, s_k_1, p_v_1, s_v_1,
            p_ret_2, s_ret_2, p_delta_2, s_delta_2, p_conv_2, s_conv_2,
            p_k_3, s_k_3, p_v_3, s_v_3)
```
